```python
import jax, jax.numpy as jnp
from jax import lax
import numpy as np

D_MODEL = 2048
BATCH = 1
SEQ = 8192
DEPTH = 2
DEC_BATCH = 32
DEC_SEQ = 8
PAST_LEN = 8192
PAGE_SIZE = 128

N_MIXERS = 2
N_SB_LAYERS = (DEPTH + N_MIXERS - 1) // N_MIXERS
N_GLA_LAYERS = DEPTH // N_MIXERS
SB_HEADS = 16
SB_HEAD_DIM = D_MODEL // SB_HEADS
SB_QBLOCK = 128
SB_BIAS_INIT = -7.0
GLA_HEADS = 4
GLA_DK = D_MODEL // 2 // GLA_HEADS
GLA_DV = D_MODEL // GLA_HEADS
GLA_GATE_RANK = 16
GLA_TAU = 16.0
GLA_CHUNK = 64
GLA_IN = 2 * GLA_HEADS * GLA_DK + 2 * GLA_HEADS * GLA_DV + GLA_GATE_RANK
D_FF = 4 * D_MODEL
EPS = 1e-6

kernel_name = 'stickbreak_gla_hybrid_step'


def rmsnorm(x, g):
    x32 = x.astype(jnp.float32)
    y = x32 * lax.rsqrt(jnp.mean(x32 * x32, axis=-1, keepdims=True) + EPS) * g.astype(jnp.float32)
    return y.astype(x.dtype)


def sq_relu_mlp(h, w_up, w_down):
    u = jnp.maximum(h @ w_up, 0)
    return (u * u) @ w_down


def sb_weights(z, allowed):
    log_1m = jnp.where(allowed, jax.nn.log_sigmoid(-z), 0.0)
    after = lax.cumsum(log_1m, axis=z.ndim - 1, reverse=True) - log_1m
    return jnp.where(allowed, jnp.exp(jax.nn.log_sigmoid(z) + after), 0.0)


def sb_attention(q, k_parts, v_parts, q_start, bias):
    B, T, H, Dh = q.shape
    blk = SB_QBLOCK if T % SB_QBLOCK == 0 else T
    nb = T // blk
    s_total = sum(k.shape[1] for k in k_parts)
    key_pos = jnp.arange(s_total)
    scale = Dh ** -0.5
    b32 = bias.astype(jnp.float32)[None, :, None, None]

    def one_block(args):
        qb, b0 = args
        z = jnp.concatenate([jnp.einsum('bqhd,bshd->bhqs', qb, k).astype(jnp.float32) for k in k_parts], axis=-1) * scale + b32
        qpos = q_start + b0 * blk + jnp.arange(blk)
        allowed = key_pos[None, :] < qpos[:, None]
        w = sb_weights(z, allowed)
        out = 0.0
        off = 0
        for v in v_parts:
            n = v.shape[1]
            out = out + jnp.einsum('bhqs,bshd->bqhd', w[..., off:off + n].astype(v.dtype), v)
            off += n
        return out

    qb = q.reshape(B, nb, blk, H, Dh).transpose(1, 0, 2, 3, 4)
    out = lax.map(one_block, (qb, jnp.arange(nb)))
    return out.transpose(1, 0, 2, 3, 4).reshape(B, T, H, Dh).astype(q.dtype)


def sb_project(h, w_qkv):
    B, T, _ = h.shape
    qkv = (h @ w_qkv).reshape(B, T, 3, SB_HEADS, SB_HEAD_DIM)
    return qkv[:, :, 0], qkv[:, :, 1], qkv[:, :, 2]


def gather_pages(pool, li, page_table):
    g = pool[li][page_table]
    b, n, p = g.shape[0], g.shape[1], g.shape[2]
    return g.reshape(b, n * p, g.shape[3], g.shape[4])


def gla_chunked(q, k, v, log_a, s0):
    B, T, H, Dk = q.shape
    Dv = v.shape[-1]
    C = GLA_CHUNK if T % GLA_CHUNK == 0 else T
    n = T // C
    f32 = jnp.float32

    def to_chunks(a):
        return a.astype(f32).reshape(B, n, C, H, a.shape[-1]).transpose(1, 0, 3, 2, 4)

    qc, kc, vc, gc = to_chunks(q), to_chunks(k), to_chunks(v), to_chunks(log_a)
    causal = jnp.tril(jnp.ones((C, C), bool))[:, :, None]

    def step(S, inp):
        qi, ki, vi, gi = inp
        G = jnp.cumsum(gi, axis=2)
        diff = G[:, :, :, None, :] - G[:, :, None, :, :]
        decay = jnp.exp(jnp.where(causal, diff, -jnp.inf))
        a = jnp.einsum('bhic,bhjc,bhijc->bhij', qi, ki, decay)
        o = jnp.einsum('bhij,bhjv->bhiv', a, vi) + jnp.einsum('bhic,bhcv->bhiv', qi * jnp.exp(G), S)
        G_last = G[:, :, -1:, :]
        S_new = jnp.exp(G_last[:, :, 0, :])[..., None] * S + jnp.einsum('bhjc,bhjv->bhcv', ki * jnp.exp(G_last - G), vi)
        return S_new, o

    S, o = lax.scan(step, s0.astype(f32), (qc, kc, vc, gc))
    o = o.transpose(1, 0, 3, 2, 4).reshape(B, T, H, Dv)
    return o, S


def gla_mixer(h, s0, w_in, w_a2, b_a, g_norm, w_o):
    B, T, _ = h.shape
    nk = GLA_HEADS * GLA_DK
    nv = GLA_HEADS * GLA_DV
    p = h @ w_in
    q, k, v, lr, r = jnp.split(p, [nk, 2 * nk, 2 * nk + nv, 2 * nk + nv + GLA_GATE_RANK], axis=-1)
    log_a = jax.nn.log_sigmoid((lr @ w_a2 + b_a).astype(jnp.float32)) / GLA_TAU
    q = q.reshape(B, T, GLA_HEADS, GLA_DK) * (GLA_DK ** -0.5)
    k = k.reshape(B, T, GLA_HEADS, GLA_DK)
    v = v.reshape(B, T, GLA_HEADS, GLA_DV)
    log_a = log_a.reshape(B, T, GLA_HEADS, GLA_DK)
    o, s = gla_chunked(q, k, v, log_a, s0)
    o = rmsnorm(o, g_norm).reshape(B, T, nv).astype(h.dtype) * jax.nn.silu(r)
    return o @ w_o, s.astype(s0.dtype)


def setup_inputs(seed: int = 0) -> dict:
    key = jax.random.key(seed)
    ks = jax.random.split(key, 20)
    f32 = jnp.float32
    n_pages = PAST_LEN // PAGE_SIZE
    n_used = DEC_BATCH * n_pages
    n_pool = n_used + max(1, n_used // 4)

    def nrm(k, shape, scale=1.0):
        return jax.random.normal(k, shape, f32) * scale

    x_prompt = nrm(ks[0], (BATCH, SEQ, D_MODEL))
    x_sample = nrm(ks[1], (DEC_BATCH, DEC_SEQ, D_MODEL))
    cache_k = nrm(ks[2], (N_SB_LAYERS, n_pool, PAGE_SIZE, SB_HEADS, SB_HEAD_DIM))
    cache_v = nrm(ks[3], (N_SB_LAYERS, n_pool, PAGE_SIZE, SB_HEADS, SB_HEAD_DIM))
    page_table = jax.random.permutation(ks[4], n_pool)[:n_used].reshape(DEC_BATCH, n_pages).astype(jnp.int32)
    state_gla = nrm(ks[5], (N_GLA_LAYERS, DEC_BATCH, GLA_HEADS, GLA_DK, GLA_DV), 2.0)
    norm_mix = 1.0 + nrm(ks[6], (DEPTH, D_MODEL), 0.02)
    sb_w_qkv = nrm(ks[7], (N_SB_LAYERS, D_MODEL, 3 * D_MODEL), D_MODEL ** -0.5)
    sb_bias = SB_BIAS_INIT + nrm(ks[18], (N_SB_LAYERS, SB_HEADS), 0.1)
    sb_w_o = nrm(ks[8], (N_SB_LAYERS, D_MODEL, D_MODEL), D_MODEL ** -0.5)
    gla_w_in = nrm(ks[9], (N_GLA_LAYERS, D_MODEL, GLA_IN), D_MODEL ** -0.5)
    gla_w_a2 = nrm(ks[10], (N_GLA_LAYERS, GLA_GATE_RANK, GLA_HEADS * GLA_DK), GLA_GATE_RANK ** -0.5)
    gla_b_a = nrm(ks[11], (N_GLA_LAYERS, GLA_HEADS * GLA_DK), 0.1)
    gla_g_norm = 1.0 + nrm(ks[12], (N_GLA_LAYERS, GLA_DV), 0.02)
    gla_w_o = nrm(ks[13], (N_GLA_LAYERS, GLA_HEADS * GLA_DV, D_MODEL), (GLA_HEADS * GLA_DV) ** -0.5)
    norm_ffn = 1.0 + nrm(ks[14], (DEPTH, D_MODEL), 0.02)
    w_up = nrm(ks[15], (DEPTH, D_MODEL, D_FF), D_MODEL ** -0.5)
    w_down = nrm(ks[16], (DEPTH, D_FF, D_MODEL), D_FF ** -0.5)
    norm_final = 1.0 + nrm(ks[17], (D_MODEL,), 0.02)
    return {'x_prompt': x_prompt, 'x_sample': x_sample, 'cache_k': cache_k, 'cache_v': cache_v,
            'page_table': page_table, 'state_gla': state_gla, 'norm_mix': norm_mix,
            'sb_w_qkv': sb_w_qkv, 'sb_bias': sb_bias, 'sb_w_o': sb_w_o, 'gla_w_in': gla_w_in, 'gla_w_a2': gla_w_a2,
            'gla_b_a': gla_b_a, 'gla_g_norm': gla_g_norm, 'gla_w_o': gla_w_o, 'norm_ffn': norm_ffn,
            'w_up': w_up, 'w_down': w_down, 'norm_final': norm_final}


def reference(x_prompt, x_sample, cache_k, cache_v, page_table, state_gla, norm_mix, sb_w_qkv, sb_bias, sb_w_o,
              gla_w_in, gla_w_a2, gla_b_a, gla_g_norm, gla_w_o, norm_ffn, w_up, w_down, norm_final):
    past_len = page_table.shape[1] * cache_k.shape[2]
    xp, xs = x_prompt, x_sample
    Bp, Tp, _ = xp.shape
    Bs, Ts, _ = xs.shape
    kp, vp, ksm, vsm, sp, ss = [], [], [], [], [], []
    for i in range(DEPTH):
        li = i // N_MIXERS
        hp = rmsnorm(xp, norm_mix[i])
        hs = rmsnorm(xs, norm_mix[i])
        if i % N_MIXERS == 0:
            q, k, v = sb_project(hp, sb_w_qkv[li])
            op = sb_attention(q, [k], [v], 0, sb_bias[li])
            q2, k2, v2 = sb_project(hs, sb_w_qkv[li])
            k_past = gather_pages(cache_k, li, page_table)
            v_past = gather_pages(cache_v, li, page_table)
            os_ = sb_attention(q2, [k_past, k2], [v_past, v2], past_len, sb_bias[li])
            yp = op.reshape(Bp, Tp, D_MODEL) @ sb_w_o[li]
            ys = os_.reshape(Bs, Ts, D_MODEL) @ sb_w_o[li]
            kp.append(k)
            vp.append(v)
            ksm.append(k2)
            vsm.append(v2)
        else:
            s0 = jnp.zeros((Bp, GLA_HEADS, GLA_DK, GLA_DV), state_gla.dtype)
            yp, s_p = gla_mixer(hp, s0, gla_w_in[li], gla_w_a2[li], gla_b_a[li], gla_g_norm[li], gla_w_o[li])
            ys, s_s = gla_mixer(hs, state_gla[li], gla_w_in[li], gla_w_a2[li], gla_b_a[li], gla_g_norm[li], gla_w_o[li])
            sp.append(s_p)
            ss.append(s_s)
        xp = xp + yp
        xs = xs + ys
        xp = xp + sq_relu_mlp(rmsnorm(xp, norm_ffn[i]), w_up[i], w_down[i])
        xs = xs + sq_relu_mlp(rmsnorm(xs, norm_ffn[i]), w_up[i], w_down[i])
    y_prompt = rmsnorm(xp, norm_final)
    y_sample = rmsnorm(xs, norm_final)
    return (y_prompt, y_sample, jnp.stack(kp), jnp.stack(vp), jnp.stack(ksm), jnp.stack(vsm), jnp.stack(sp), jnp.stack(ss))
```

```python
import functools

import numpy as np
import jax
import jax.numpy as jnp
from jax import lax
from jax.experimental import pallas as pl
from jax.experimental.pallas import tpu as pltpu

F32 = jnp.float32
BF16 = jnp.bfloat16

EPS = 1e-6
SB_HEADS = 16
SB_HEAD_DIM = 128
GLA_HEADS = 4
GLA_DK = 256
GLA_DV = 512
GLA_GATE_RANK = 16
GLA_TAU = 16.0
GLA_CHUNK = 64
GLA_SUB = 16
LANES = 128
VMEM_LIMIT_BYTES = 56 * 1024 * 1024


def _cparams(n_grid_dims):
    return pltpu.CompilerParams(
        dimension_semantics=("arbitrary",) * n_grid_dims,
        vmem_limit_bytes=VMEM_LIMIT_BYTES)


def _neg_softplus(z):
    return jnp.minimum(-z, 0.0) - jnp.log1p(jnp.exp(-jnp.abs(z)))


def _rms_scale(x):
    return lax.rsqrt(jnp.mean(x * x, axis=-1, keepdims=True) + EPS)


def _proj_kernel(*refs, has_norm, has_res, col_scale, n_out):
    it = iter(refs)
    x_ref = next(it)
    g_ref = next(it) if has_norm else None
    w_ref = next(it)
    res_ref = next(it) if has_res else None
    out_refs = [next(it) for _ in range(n_out)]
    h_ref = next(it) if has_norm else None

    if has_norm:
        @pl.when(pl.program_id(1) == 0)
        def _():
            x = x_ref[...]
            h_ref[...] = (x * _rms_scale(x) * g_ref[...]).astype(BF16)
        h = h_ref[...]
    else:
        h = x_ref[...].astype(BF16)
    y = jnp.dot(h, w_ref[...], preferred_element_type=F32)
    if col_scale is not None:
        y = y * col_scale
    if has_res:
        y = y + res_ref[...]
    for o_ref in out_refs:
        o_ref[...] = y.astype(o_ref.dtype)


def _proj(x, w, *, gain=None, res=None, col_scale=None, out_dtypes=(F32,), tm=512, tn=1024, name="proj"):
    m, kdim = x.shape
    n = w.shape[1]
    tm = min(tm, m)
    tn = min(tn, n)
    assert m % tm == 0 and n % tn == 0
    has_norm = gain is not None
    has_res = res is not None
    in_specs = [pl.BlockSpec((tm, kdim), lambda i, j: (i, 0))]
    args = [x]
    if has_norm:
        in_specs.append(pl.BlockSpec((1, kdim), lambda i, j: (0, 0)))
        args.append(gain.reshape(1, kdim).astype(F32))
    in_specs.append(pl.BlockSpec((kdim, tn), lambda i, j: (0, j)))
    args.append(w)
    if has_res:
        in_specs.append(pl.BlockSpec((tm, tn), lambda i, j: (i, j)))
        args.append(res)
    out_shape = [jax.ShapeDtypeStruct((m, n), dt) for dt in out_dtypes]
    out_specs = [pl.BlockSpec((tm, tn), lambda i, j: (i, j)) for _ in out_dtypes]
    scratch = [pltpu.VMEM((tm, kdim), BF16)] if has_norm else []
    outs = pl.pallas_call(
        functools.partial(_proj_kernel, has_norm=has_norm, has_res=has_res,
                          col_scale=col_scale, n_out=len(out_dtypes)),
        grid=(m // tm, n // tn),
        in_specs=in_specs, out_specs=out_specs, out_shape=out_shape,
        scratch_shapes=scratch, compiler_params=_cparams(2), name=name,
    )(*args)
    return outs


def _mlp_kernel(x_ref, g_ref, wu_ref, wd_ref, gf_ref, o_ref, h_ref, *, final_norm):
    c = pl.program_id(1)

    @pl.when(c == 0)
    def _():
        x = x_ref[...]
        h_ref[...] = (x * _rms_scale(x) * g_ref[...]).astype(BF16)
        o_ref[...] = x

    u = jnp.maximum(jnp.dot(h_ref[...], wu_ref[...], preferred_element_type=F32), 0.0)
    o_ref[...] += jnp.dot((u * u).astype(BF16), wd_ref[...], preferred_element_type=F32)

    if final_norm:
        @pl.when(c == pl.num_programs(1) - 1)
        def _():
            y = o_ref[...]
            o_ref[...] = y * _rms_scale(y) * gf_ref[...]


def _mlp(x, gain, w_up, w_down, gain_final, *, final_norm, tm=512, tf=1024, name="mlp"):
    m, d = x.shape
    dff = w_up.shape[1]
    tm = min(tm, m)
    assert m % tm == 0 and dff % tf == 0
    return pl.pallas_call(
        functools.partial(_mlp_kernel, final_norm=final_norm),
        grid=(m // tm, dff // tf),
        in_specs=[
            pl.BlockSpec((tm, d), lambda i, c: (i, 0)),
            pl.BlockSpec((1, d), lambda i, c: (0, 0)),
            pl.BlockSpec((d, tf), lambda i, c: (0, c)),
            pl.BlockSpec((tf, d), lambda i, c: (c, 0)),
            pl.BlockSpec((1, d), lambda i, c: (0, 0)),
        ],
        out_specs=pl.BlockSpec((tm, d), lambda i, c: (i, 0)),
        out_shape=jax.ShapeDtypeStruct((m, d), F32),
        scratch_shapes=[pltpu.VMEM((tm, d), BF16)],
        compiler_params=_cparams(2), name=name,
    )(x, gain.reshape(1, d).astype(F32), w_up, w_down, gain_final.reshape(1, d).astype(F32))


def _sb_block(q, kb, vb, tri, bias, carry, allowed):
    z = lax.dot_general(q, kb, (((1,), (1,)), ((), ())), preferred_element_type=F32) + bias
    lneg = _neg_softplus(z)
    if allowed is not None:
        lneg = jnp.where(allowed, lneg, 0.0)
    cum = jnp.dot(lneg.astype(BF16), tri, preferred_element_type=F32)
    w = jnp.exp(z + cum + carry)
    if allowed is not None:
        w = jnp.where(allowed, w, 0.0)
    pv = jnp.dot(w.astype(BF16), vb, preferred_element_type=F32)
    return pv, carry + cum[:, 0:1]


def _sb_prompt_kernel(bias_ref, q_ref, k_ref, v_ref, tri_ref, o_ref, *, blk):
    h = pl.program_id(0)
    qi = pl.program_id(1)
    bias = bias_ref[h]
    q = q_ref[...]
    tri = tri_ref[...]
    row = lax.broadcasted_iota(jnp.int32, (blk, blk), 0)
    col = lax.broadcasted_iota(jnp.int32, (blk, blk), 1)
    d0 = pl.multiple_of(qi * blk, blk)
    acc, carry = _sb_block(q, k_ref[pl.ds(d0, blk), :], v_ref[pl.ds(d0, blk), :], tri, bias,
                           jnp.zeros((blk, 1), F32), col < row)

    def body(t, state):
        acc, carry = state
        k0 = pl.multiple_of((qi - 1 - t) * blk, blk)
        pv, carry = _sb_block(q, k_ref[pl.ds(k0, blk), :], v_ref[pl.ds(k0, blk), :], tri, bias, carry, None)
        return acc + pv, carry

    acc, _ = lax.fori_loop(0, qi, body, (acc, carry))
    o_ref[...] = acc.astype(o_ref.dtype)


def _tri_ge(n):
    i = np.arange(n)
    return jnp.asarray((i[:, None] >= i[None, :]).astype(np.float32), dtype=BF16)


def _sb_attention_prompt(q, k, v, bias, *, blk=256):
    t = q.shape[0]
    dh = SB_HEAD_DIM
    return pl.pallas_call(
        functools.partial(_sb_prompt_kernel, blk=blk),
        grid=(SB_HEADS, t // blk),
        in_specs=[
            pl.BlockSpec(memory_space=pltpu.SMEM),
            pl.BlockSpec((blk, dh), lambda h, i: (i, h)),
            pl.BlockSpec((t, dh), lambda h, i: (0, h)),
            pl.BlockSpec((t, dh), lambda h, i: (0, h)),
            pl.BlockSpec((blk, blk), lambda h, i: (0, 0)),
        ],
        out_specs=pl.BlockSpec((blk, dh), lambda h, i: (i, h)),
        out_shape=jax.ShapeDtypeStruct((t, SB_HEADS * dh), BF16),
        compiler_params=_cparams(2), name="sb_attn_prompt",
    )(bias.astype(F32), q, k, v, _tri_ge(blk))


def _sb_sample_kernel(pt_ref, q_ref, k2_ref, v2_ref, kc_ref, vc_ref, tri_ref, bias_ref, o_ref,
                      acc_ref, carry_ref, *, n_q, page):
    s = pl.program_id(1)
    dh = SB_HEAD_DIM
    nrow = SB_HEADS * n_q
    pair = 2 * n_q

    q_rows = jnp.concatenate([q_ref[:, h * dh:(h + 1) * dh] for h in range(SB_HEADS)], axis=0).astype(BF16)

    def step(k_of_head, v_of_head, allowed):
        zs = []
        for h in range(SB_HEADS):
            g = h // 2
            zz = lax.dot_general(q_rows[g * pair:(g + 1) * pair], k_of_head(h), (((1,), (1,)), ((), ())),
                                 preferred_element_type=F32)
            zs.append(zz[(h % 2) * n_q:(h % 2 + 1) * n_q])
        z = jnp.concatenate(zs, axis=0) + bias_ref[...]
        lneg = _neg_softplus(z)
        if allowed is not None:
            lneg = jnp.where(allowed, lneg, 0.0)
        cum = jnp.dot(lneg.astype(BF16), tri_ref[...], preferred_element_type=F32)
        w = jnp.exp(z + cum + carry_ref[...])
        if allowed is not None:
            w = jnp.where(allowed, w, 0.0)
        carry_ref[...] += cum[:, 0:1]
        wb = w.astype(BF16)
        for h in range(SB_HEADS):
            g = h // 2
            pv = jnp.dot(wb[g * pair:(g + 1) * pair], v_of_head(h), preferred_element_type=F32)
            acc_ref[h * n_q:(h + 1) * n_q, :] += pv[(h % 2) * n_q:(h % 2 + 1) * n_q]

    @pl.when(s == 0)
    def _():
        acc_ref[...] = jnp.zeros_like(acc_ref)
        carry_ref[...] = jnp.zeros_like(carry_ref)
        pad = jnp.zeros((page - n_q, dh), BF16)
        row = lax.broadcasted_iota(jnp.int32, (nrow, page), 0)
        col = lax.broadcasted_iota(jnp.int32, (nrow, page), 1)
        allowed = col < (row % n_q)
        step(lambda h: jnp.concatenate([k2_ref[:, h * dh:(h + 1) * dh].astype(BF16), pad], axis=0),
             lambda h: jnp.concatenate([v2_ref[:, h * dh:(h + 1) * dh].astype(BF16), pad], axis=0),
             allowed)

    @pl.when(s > 0)
    def _():
        step(lambda h: kc_ref[:, h, :].astype(BF16),
             lambda h: vc_ref[:, h, :].astype(BF16),
             None)

    @pl.when(s == pl.num_programs(1) - 1)
    def _():
        for h in range(SB_HEADS):
            o_ref[:, h * dh:(h + 1) * dh] = acc_ref[h * n_q:(h + 1) * n_q, :]


def _sb_attention_sample(q, k2, v2, cache_k, cache_v, page_table, bias, *, n_q):
    nb, n_pages = page_table.shape
    page = cache_k.shape[1]
    hd = SB_HEADS * SB_HEAD_DIM
    nrow = SB_HEADS * n_q

    def cache_map(b, s, pt):
        return (pt[b, n_pages - jnp.maximum(s, 1)], 0, 0, 0)

    bias_rows = jnp.broadcast_to(jnp.repeat(bias.astype(F32), n_q)[:, None], (nrow, page))
    grid_spec = pltpu.PrefetchScalarGridSpec(
        num_scalar_prefetch=1,
        grid=(nb, n_pages + 1),
        in_specs=[
            pl.BlockSpec((n_q, hd), lambda b, s, pt: (b, 0)),
            pl.BlockSpec((n_q, hd), lambda b, s, pt: (b, 0)),
            pl.BlockSpec((n_q, hd), lambda b, s, pt: (b, 0)),
            pl.BlockSpec((None, page, SB_HEADS, SB_HEAD_DIM), cache_map),
            pl.BlockSpec((None, page, SB_HEADS, SB_HEAD_DIM), cache_map),
            pl.BlockSpec((page, page), lambda b, s, pt: (0, 0)),
            pl.BlockSpec((nrow, page), lambda b, s, pt: (0, 0)),
        ],
        out_specs=pl.BlockSpec((n_q, hd), lambda b, s, pt: (b, 0)),
        scratch_shapes=[pltpu.VMEM((nrow, SB_HEAD_DIM), F32), pltpu.VMEM((nrow, 1), F32)],
    )
    return pl.pallas_call(
        functools.partial(_sb_sample_kernel, n_q=n_q, page=page),
        grid_spec=grid_spec,
        out_shape=jax.ShapeDtypeStruct((nb * n_q, hd), F32),
        compiler_params=_cparams(2), name="sb_attn_sample",
    )(page_table, q, k2, v2, cache_k, cache_v, _tri_ge(page), bias_rows)


def _gla_masks(r, c, sc):
    i = np.arange(r)
    chunk = i // c
    cstart = chunk * c
    sub0 = (i // sc) * sc
    same = chunk[:, None] == chunk[None, :]
    il, ll = i[:, None], i[None, :]
    nb = c // sc
    mats = [
        same & (ll <= il),
        (sub0[:, None] < ll) & (ll <= il),
        same & (il < ll),
    ]
    for blk in range(1, nb):
        mats.append(same & (il < ll) & (ll <= (cstart + sc * blk)[:, None]))
    sums = np.concatenate(mats, axis=0).astype(np.float32)
    sel = []
    for blk in range(1, nb):
        sel.append(same & (((i % c) // sc) == blk)[:, None] & (ll < (cstart + sc * blk)[:, None]))
    sel = np.stack(sel, axis=0).astype(np.float32) if sel else np.zeros((1, r, r), np.float32)
    return jnp.asarray(sums, dtype=BF16), jnp.asarray(sel, dtype=F32)


def _gla_prompt_kernel(qk_ref, v_ref, r_ref, lr_ref, wa_ref, ba_ref, gn_ref, sums_ref, sel_ref,
                       o_ref, s_out_ref, s_ref, g_scr, k_scr, a_scr, *, rows, chunk, sub):
    t = pl.program_id(0)
    nk = GLA_HEADS * GLA_DK
    n_chunks = rows // chunk
    n_blk = chunk // sub
    n_sub = rows // sub

    @pl.when(t == 0)
    def _():
        s_ref[...] = jnp.zeros_like(s_ref)

    lr = lr_ref[...].astype(BF16)
    for h in range(GLA_HEADS):
        ksl = slice(h * GLA_DK, (h + 1) * GLA_DK)
        vsl = slice(h * GLA_DV, (h + 1) * GLA_DV)
        x = jnp.dot(lr, wa_ref[:, ksl], preferred_element_type=F32) + ba_ref[:, ksl]
        g = _neg_softplus(-x) * (1.0 / GLA_TAU)
        g_hi = g.astype(BF16)
        g_lo = (g - g_hi.astype(F32)).astype(BF16)
        e = (jnp.dot(sums_ref[...], g_hi, preferred_element_type=F32)
             + jnp.dot(sums_ref[...], g_lo, preferred_element_type=F32))
        gcum = e[0:rows]
        qs = qk_ref[:, ksl].astype(F32) * (GLA_DK ** -0.5)
        kf = qk_ref[:, nk + h * GLA_DK: nk + (h + 1) * GLA_DK].astype(F32)
        vb = v_ref[:, vsl]
        q_hat = (qs * jnp.exp(gcum)).astype(BF16)
        q_til = (qs * jnp.exp(e[rows:2 * rows])).astype(BF16)
        k_hat = (kf * jnp.exp(e[2 * rows:3 * rows])).astype(BF16)

        a = jnp.zeros((rows, rows), F32)
        for b in range(1, n_blk):
            k_til = (kf * jnp.exp(e[(2 + b) * rows:(3 + b) * rows])).astype(BF16)
            p = lax.dot_general(q_til, k_til, (((1,), (1,)), ((), ())), preferred_element_type=F32)
            a = a + p * sel_ref[b - 1]
        a_scr[...] = a

        g_scr[...] = gcum
        k_scr[...] = kf
        lane = lax.broadcasted_iota(jnp.int32, (sub, LANES), 1)
        rowi = lax.broadcasted_iota(jnp.int32, (sub, 1), 0)
        for sci in range(n_sub):
            r0 = sci * sub
            q_s = qs[r0:r0 + sub]
            g_s = gcum[r0:r0 + sub]
            lane0 = r0 % LANES
            tile = r0 // LANES
            blk_a = jnp.zeros((sub, LANES), F32)
            for j in range(sub):
                kj = k_scr[r0 + j:r0 + j + 1, :]
                gj = g_scr[r0 + j:r0 + j + 1, :]
                tt = q_s * kj * jnp.exp(jnp.minimum(g_s - gj, 0.0))
                colv = jnp.sum(tt, axis=-1, keepdims=True)
                colv = jnp.where(rowi >= j, colv, 0.0)
                blk_a = jnp.where(lane == lane0 + j, colv, blk_a)
            a_scr[r0:r0 + sub, tile * LANES:(tile + 1) * LANES] += blk_a

        o_intra = jnp.dot(a_scr[...].astype(BF16), vb, preferred_element_type=F32)

        gcum_t = gcum.T
        outs = []
        for c in range(n_chunks):
            c0 = c * chunk
            s_old = s_ref[h]
            o_c = o_intra[c0:c0 + chunk] + jnp.dot(q_hat[c0:c0 + chunk], s_old.astype(BF16),
                                                  preferred_element_type=F32)
            decay = jnp.exp(gcum_t[:, c0 + chunk - 1:c0 + chunk])
            kv = lax.dot_general(k_hat[c0:c0 + chunk], vb[c0:c0 + chunk], (((0,), (0,)), ((), ())),
                                 preferred_element_type=F32)
            s_ref[h] = decay * s_old + kv
            outs.append(o_c)
        o_h = jnp.concatenate(outs, axis=0) if n_chunks > 1 else outs[0]
        o_n = o_h * _rms_scale(o_h) * gn_ref[...]
        r_h = r_ref[:, vsl].astype(F32)
        o_ref[:, vsl] = (o_n * (r_h / (1.0 + jnp.exp(-r_h)))).astype(o_ref.dtype)

    @pl.when(t == pl.num_programs(0) - 1)
    def _():
        s_out_ref[...] = s_ref[...]


def _gla_prompt(p, lr, w_a2p, b_a, g_norm, *, rows=256):
    t = p.shape[0]
    nv = GLA_HEADS * GLA_DV
    sums, sel = _gla_masks(rows, GLA_CHUNK, GLA_SUB)
    n_sums = sums.shape[0]
    o, s = pl.pallas_call(
        functools.partial(_gla_prompt_kernel, rows=rows, chunk=GLA_CHUNK, sub=GLA_SUB),
        grid=(t // rows,),
        in_specs=[
            pl.BlockSpec((rows, nv), lambda i: (i, 0)),
            pl.BlockSpec((rows, nv), lambda i: (i, 1)),
            pl.BlockSpec((rows, nv), lambda i: (i, 2)),
            pl.BlockSpec((rows, LANES), lambda i: (i, 0)),
            pl.BlockSpec((LANES, GLA_HEADS * GLA_DK), lambda i: (0, 0)),
            pl.BlockSpec((1, GLA_HEADS * GLA_DK), lambda i: (0, 0)),
            pl.BlockSpec((1, GLA_DV), lambda i: (0, 0)),
            pl.BlockSpec((n_sums, rows), lambda i: (0, 0)),
            pl.BlockSpec(sel.shape, lambda i: (0, 0, 0)),
        ],
        out_specs=[
            pl.BlockSpec((rows, nv), lambda i: (i, 0)),
            pl.BlockSpec((GLA_HEADS, GLA_DK, GLA_DV), lambda i: (0, 0, 0)),
        ],
        out_shape=[
            jax.ShapeDtypeStruct((t, nv), BF16),
            jax.ShapeDtypeStruct((GLA_HEADS, GLA_DK, GLA_DV), F32),
        ],
        scratch_shapes=[
            pltpu.VMEM((GLA_HEADS, GLA_DK, GLA_DV), F32),
            pltpu.VMEM((rows, GLA_DK), F32),
            pltpu.VMEM((rows, GLA_DK), F32),
            pltpu.VMEM((rows, rows), F32),
        ],
        compiler_params=_cparams(1), name="gla_prompt",
    )(p, p, p, lr, w_a2p, b_a.reshape(1, -1).astype(F32), g_norm.reshape(1, -1).astype(F32), sums, sel)
    return o, s


def _gla_sample_kernel(q_ref, k_ref, v_ref, r_ref, lr_ref, wa_ref, ba_ref, gn_ref, s0_ref, o_ref, s_out_ref,
                       *, n_t):
    pad = jnp.zeros((n_t, LANES), F32)
    lr16 = jnp.concatenate([lr_ref[...], pad], axis=0).astype(BF16)
    x = jnp.dot(lr16, wa_ref[...], preferred_element_type=F32)[0:n_t] + ba_ref[...]
    g = _neg_softplus(-x) * (1.0 / GLA_TAU)
    rowi = lax.broadcasted_iota(jnp.int32, (n_t, 1), 0)
    gcum = jnp.zeros_like(g)
    for l in range(n_t):
        gcum = gcum + jnp.where(rowi >= l, g[l:l + 1, :], 0.0)
    qs = q_ref[...].astype(F32) * (GLA_DK ** -0.5)
    kf = k_ref[...].astype(F32)
    vf = v_ref[...].astype(F32)
    s_old = s0_ref[...]

    o = jnp.zeros((n_t, GLA_DV), F32)
    for j in range(n_t):
        tt = qs * kf[j:j + 1, :] * jnp.exp(jnp.minimum(gcum - gcum[j:j + 1, :], 0.0))
        colv = jnp.where(rowi >= j, jnp.sum(tt, axis=-1, keepdims=True), 0.0)
        o = o + colv * vf[j:j + 1, :]

    zpad = jnp.zeros((n_t, GLA_DK), F32)
    q_hat = jnp.concatenate([qs * jnp.exp(gcum), zpad], axis=0).astype(BF16)
    o = o + jnp.dot(q_hat, s_old.astype(BF16), preferred_element_type=F32)[0:n_t]

    g_last = gcum[n_t - 1:n_t, :]
    k_hat = jnp.concatenate([kf * jnp.exp(g_last - gcum), zpad], axis=0).astype(BF16)
    v16 = jnp.concatenate([vf, jnp.zeros((n_t, GLA_DV), F32)], axis=0).astype(BF16)
    kv = lax.dot_general(k_hat, v16, (((0,), (0,)), ((), ())), preferred_element_type=F32)
    decay = jnp.exp(jnp.broadcast_to(g_last, (LANES, GLA_DK)).T[:, 0:1])
    s_out_ref[...] = decay * s_old + kv

    o_n = o * _rms_scale(o) * gn_ref[...]
    r_h = r_ref[...].astype(F32)
    o_ref[...] = o_n * (r_h / (1.0 + jnp.exp(-r_h)))


def _gla_sample(p, lr, w_a2p, b_a, g_norm, s0, *, n_t, n_seq):
    nb = n_seq
    nh = GLA_HEADS
    o, s = pl.pallas_call(
        functools.partial(_gla_sample_kernel, n_t=n_t),
        grid=(nb, GLA_HEADS),
        in_specs=[
            pl.BlockSpec((n_t, GLA_DK), lambda b, h: (b, h)),
            pl.BlockSpec((n_t, GLA_DK), lambda b, h: (b, nh + h)),
            pl.BlockSpec((n_t, GLA_DV), lambda b, h: (b, nh + h)),
            pl.BlockSpec((n_t, GLA_DV), lambda b, h: (b, 2 * nh + h)),
            pl.BlockSpec((n_t, LANES), lambda b, h: (b, 0)),
            pl.BlockSpec((LANES, GLA_DK), lambda b, h: (0, h)),
            pl.BlockSpec((1, GLA_DK), lambda b, h: (0, h)),
            pl.BlockSpec((1, GLA_DV), lambda b, h: (0, 0)),
            pl.BlockSpec((None, None, GLA_DK, GLA_DV), lambda b, h: (b, h, 0, 0)),
        ],
        out_specs=[
            pl.BlockSpec((n_t, GLA_DV), lambda b, h: (b, h)),
            pl.BlockSpec((None, None, GLA_DK, GLA_DV), lambda b, h: (b, h, 0, 0)),
        ],
        out_shape=[
            jax.ShapeDtypeStruct((nb * n_t, GLA_HEADS * GLA_DV), F32),
            jax.ShapeDtypeStruct((nb,) + s0.shape[1:], F32),
        ],
        compiler_params=_cparams(2), name="gla_sample",
    )(p, p, p, p, lr, w_a2p, b_a.reshape(1, -1).astype(F32), g_norm.reshape(1, -1).astype(F32), s0)
    return o, s


def kernel(x_prompt, x_sample, cache_k, cache_v, page_table, state_gla, norm_mix, sb_w_qkv, sb_bias, sb_w_o,
           gla_w_in, gla_w_a2, gla_b_a, gla_g_norm, gla_w_o, norm_ffn, w_up, w_down, norm_final):
    bp, tp, d = x_prompt.shape
    bs, ts, _ = x_sample.shape
    assert bp == 1
    hd = SB_HEADS * SB_HEAD_DIM
    xp = x_prompt.reshape(bp * tp, d)
    xs = x_sample.reshape(bs * ts, d)
    scale = SB_HEAD_DIM ** -0.5
    nk = GLA_HEADS * GLA_DK
    nv = GLA_HEADS * GLA_DV

    wqkv = sb_w_qkv[0].astype(BF16)
    wq, wk, wv = wqkv[:, :hd], wqkv[:, hd:2 * hd], wqkv[:, 2 * hd:]
    wo = sb_w_o[0].astype(BF16)
    g0 = norm_mix[0]

    (q_p,) = _proj(xp, wq, gain=g0, col_scale=scale, out_dtypes=(BF16,), name="sb_q_prompt")
    k_p, k_pb = _proj(xp, wk, gain=g0, out_dtypes=(F32, BF16), name="sb_k_prompt")
    v_p, v_pb = _proj(xp, wv, gain=g0, out_dtypes=(F32, BF16), name="sb_v_prompt")
    o_p = _sb_attention_prompt(q_p, k_pb, v_pb, sb_bias[0])
    (xp,) = _proj(o_p, wo, res=xp, name="sb_o_prompt")

    (q_s,) = _proj(xs, wq, gain=g0, col_scale=scale, name="sb_q_sample")
    (k_s,) = _proj(xs, wk, gain=g0, name="sb_k_sample")
    (v_s,) = _proj(xs, wv, gain=g0, name="sb_v_sample")
    page = cache_k.shape[2]
    o_s = _sb_attention_sample(q_s, k_s, v_s, cache_k.reshape(-1, page, SB_HEADS, SB_HEAD_DIM),
                               cache_v.reshape(-1, page, SB_HEADS, SB_HEAD_DIM), page_table, sb_bias[0], n_q=ts)
    (xs,) = _proj(o_s, wo, res=xs, name="sb_o_sample")

    wu0, wd0 = w_up[0].astype(BF16), w_down[0].astype(BF16)
    xp = _mlp(xp, norm_ffn[0], wu0, wd0, norm_final, final_norm=False, name="mlp0_prompt")
    xs = _mlp(xs, norm_ffn[0], wu0, wd0, norm_final, final_norm=False, name="mlp0_sample")

    w_in = gla_w_in[0]
    w_main = jnp.concatenate([w_in[:, :2 * nk + nv], w_in[:, 2 * nk + nv + GLA_GATE_RANK:]], axis=1).astype(BF16)
    w_lr = jnp.pad(w_in[:, 2 * nk + nv:2 * nk + nv + GLA_GATE_RANK],
                   ((0, 0), (0, LANES - GLA_GATE_RANK))).astype(BF16)
    w_a2p = jnp.pad(gla_w_a2[0], ((0, LANES - GLA_GATE_RANK), (0, 0))).astype(BF16)
    w_go = gla_w_o[0].astype(BF16)
    g1 = norm_mix[1]

    (p_p,) = _proj(xp, w_main, gain=g1, out_dtypes=(BF16,), name="gla_in_prompt")
    (lr_p,) = _proj(xp, w_lr, gain=g1, name="gla_lr_prompt")
    go_p, s_p = _gla_prompt(p_p, lr_p, w_a2p, gla_b_a[0], gla_g_norm[0])
    (xp,) = _proj(go_p, w_go, res=xp, name="gla_o_prompt")

    (p_s,) = _proj(xs, w_main, gain=g1, name="gla_in_sample")
    (lr_s,) = _proj(xs, w_lr, gain=g1, name="gla_lr_sample")
    s0 = state_gla.reshape(-1, GLA_HEADS, GLA_DK, GLA_DV)
    go_s, s_s = _gla_sample(p_s, lr_s, w_a2p, gla_b_a[0], gla_g_norm[0], s0, n_t=ts, n_seq=bs)
    (xs,) = _proj(go_s, w_go, res=xs, name="gla_o_sample")

    wu1, wd1 = w_up[1].astype(BF16), w_down[1].astype(BF16)
    y_p = _mlp(xp, norm_ffn[1], wu1, wd1, norm_final, final_norm=True, name="mlp1_prompt")
    y_s = _mlp(xs, norm_ffn[1], wu1, wd1, norm_final, final_norm=True, name="mlp1_sample")

    return (
        y_p.reshape(bp, tp, d),
        y_s.reshape(bs, ts, d),
        k_p.reshape(1, bp, tp, SB_HEADS, SB_HEAD_DIM),
        v_p.reshape(1, bp, tp, SB_HEADS, SB_HEAD_DIM),
        k_s.reshape(1, bs, ts, SB_HEADS, SB_HEAD_DIM),
        v_s.reshape(1, bs, ts, SB_HEADS, SB_HEAD_DIM),
        s_p.reshape(1, bp, GLA_HEADS, GLA_DK, GLA_DV),
        s_s.reshape(1, bs, GLA_HEADS, GLA_DK, GLA_DV),
    )
```

```python
import functools

import numpy as np
import jax
import jax.numpy as jnp
from jax import lax
from jax.experimental import pallas as pl
from jax.experimental.pallas import tpu as pltpu

F32 = jnp.float32
BF16 = jnp.bfloat16

EPS = 1e-6
SB_HEADS = 16
SB_HEAD_DIM = 128
GLA_HEADS = 4
GLA_DK = 256
GLA_DV = 512
GLA_GATE_RANK = 16
GLA_TAU = 16.0
GLA_CHUNK = 64
GLA_SUB = 16
LANES = 128
LOG2E = 1.4426950408889634
VMEM_LIMIT_BYTES = 56 * 1024 * 1024


def _cparams(n_grid_dims):
    return pltpu.CompilerParams(
        dimension_semantics=("arbitrary",) * n_grid_dims,
        vmem_limit_bytes=VMEM_LIMIT_BYTES)


def _log2_gate(x):
    softplus_neg = jnp.maximum(-x, 0.0) + jnp.log(1.0 + jnp.exp(-jnp.abs(x)))
    return softplus_neg * (-LOG2E / GLA_TAU)


def _rms_scale(x):
    return lax.rsqrt(jnp.mean(x * x, axis=-1, keepdims=True) + EPS)


def _proj_kernel(*refs, has_norm, has_res, sections, n_tiles):
    it = iter(refs)
    x_ref = next(it)
    g_ref = next(it) if has_norm else None
    w_ref = next(it)
    res_ref = next(it) if has_res else None
    out_refs = [next(it) for _ in sections]
    h_ref = next(it) if has_norm else None
    j = pl.program_id(1)

    if has_norm:
        @pl.when(j == 0)
        def _():
            x = x_ref[...]
            h_ref[...] = (x * _rms_scale(x) * g_ref[...]).astype(BF16)
        h = h_ref[...]
    else:
        h = x_ref[...].astype(BF16)
    y = jnp.dot(h, w_ref[...], preferred_element_type=F32)
    if has_res:
        y = y + res_ref[...]
    for (first, count, scale), o_ref in zip(sections, out_refs):
        def write(o_ref=o_ref, scale=scale):
            o_ref[...] = (y if scale is None else y * scale).astype(o_ref.dtype)
        if first == 0 and count == n_tiles:
            write()
        else:
            pl.when((j >= first) & (j < first + count))(write)


def _proj(x, w, *, gain=None, res=None, outs=None, layer=None, tm=512, tn=1024, name="proj"):
    m, kdim = x.shape
    n = w.shape[-1]
    tm = min(tm, m)
    tn = min(tn, n)
    assert m % tm == 0 and n % tn == 0
    n_tiles = n // tn
    outs = outs or [(F32, 0, n, None)]
    has_norm = gain is not None
    has_res = res is not None
    assert not has_res or (len(outs) == 1 and outs[0][1] == 0 and outs[0][2] == n)
    in_specs = [pl.BlockSpec((tm, kdim), lambda i, j: (i, 0))]
    args = [x]
    if has_norm:
        in_specs.append(pl.BlockSpec((1, kdim), lambda i, j: (0, 0)))
        args.append(gain.reshape(1, kdim).astype(F32))
    if layer is None:
        in_specs.append(pl.BlockSpec((kdim, tn), lambda i, j: (0, j)))
    else:
        in_specs.append(pl.BlockSpec((None, kdim, tn), lambda i, j: (layer, 0, j)))
    args.append(w)
    if has_res:
        in_specs.append(pl.BlockSpec((tm, tn), lambda i, j: (i, j)))
        args.append(res)
    sections, out_shape, out_specs = [], [], []
    for dt, col0, cols, scale in outs:
        assert col0 % tn == 0 and cols % tn == 0
        first, count = col0 // tn, cols // tn
        sections.append((first, count, scale))
        out_shape.append(jax.ShapeDtypeStruct((m, cols), dt))
        out_specs.append(pl.BlockSpec(
            (tm, tn), lambda i, j, first=first, count=count: (i, jnp.clip(j - first, 0, count - 1))))
    scratch = [pltpu.VMEM((tm, kdim), BF16)] if has_norm else []
    return pl.pallas_call(
        functools.partial(_proj_kernel, has_norm=has_norm, has_res=has_res,
                          sections=tuple(sections), n_tiles=n_tiles),
        grid=(m // tm, n_tiles),
        in_specs=in_specs, out_specs=out_specs, out_shape=out_shape,
        scratch_shapes=scratch, compiler_params=_cparams(2), name=name,
    )(*args)


def _mlp_kernel(x_ref, g_ref, wu_ref, wd_ref, gf_ref, o_ref, h_ref, *, final_norm):
    c = pl.program_id(1)

    @pl.when(c == 0)
    def _():
        x = x_ref[...]
        h_ref[...] = (x * _rms_scale(x) * g_ref[...]).astype(BF16)
        o_ref[...] = x

    u = jnp.maximum(jnp.dot(h_ref[...], wu_ref[...], preferred_element_type=F32), 0.0)
    o_ref[...] += jnp.dot((u * u).astype(BF16), wd_ref[...], preferred_element_type=F32)

    if final_norm:
        @pl.when(c == pl.num_programs(1) - 1)
        def _():
            y = o_ref[...]
            o_ref[...] = y * _rms_scale(y) * gf_ref[...]


def _mlp(x, gain, w_up, w_down, layer, gain_final, *, final_norm, tm=512, tf=1024, name="mlp"):
    m, d = x.shape
    dff = w_up.shape[2]
    tm = min(tm, m)
    assert m % tm == 0 and dff % tf == 0
    return pl.pallas_call(
        functools.partial(_mlp_kernel, final_norm=final_norm),
        grid=(m // tm, dff // tf),
        in_specs=[
            pl.BlockSpec((tm, d), lambda i, c: (i, 0)),
            pl.BlockSpec((1, d), lambda i, c: (0, 0)),
            pl.BlockSpec((None, d, tf), lambda i, c: (layer, 0, c)),
            pl.BlockSpec((None, tf, d), lambda i, c: (layer, c, 0)),
            pl.BlockSpec((1, d), lambda i, c: (0, 0)),
        ],
        out_specs=pl.BlockSpec((tm, d), lambda i, c: (i, 0)),
        out_shape=jax.ShapeDtypeStruct((m, d), F32),
        scratch_shapes=[pltpu.VMEM((tm, d), BF16)],
        compiler_params=_cparams(2), name=name,
    )(x, gain.reshape(1, d).astype(F32), w_up, w_down, gain_final.reshape(1, d).astype(F32))


def _softplus2(z2):
    neg_abs = lax.bitcast_convert_type(lax.bitcast_convert_type(z2, jnp.uint32) | jnp.uint32(0x80000000), F32)
    return jnp.maximum(z2, 0.0) + jnp.log2(1.0 + jnp.exp2(neg_abs))


def _sb_block(q, kb, vb, ntri, bias2, carry, allowed_top):
    def mask_top(x):
        if allowed_top is None:
            return x
        nt = allowed_top.shape[0]
        top = jnp.where(allowed_top, x[:nt], 0.0)
        return top if x.shape[0] == nt else jnp.concatenate([top, x[nt:]], axis=0)

    z = lax.dot_general(q, kb, (((1,), (1,)), ((), ())), preferred_element_type=F32) + bias2
    sp = mask_top(_softplus2(z))
    cum = jnp.dot(sp.astype(BF16), ntri, preferred_element_type=F32)
    w = mask_top(jnp.exp2(z + cum + carry))
    pv = jnp.dot(w.astype(BF16), vb, preferred_element_type=F32)
    return pv, carry + cum[:, 0:1]


def _sb_prompt_kernel(bias_ref, q_ref, k_ref, v_ref, ntri_ref, o_ref, acc_ref, carry_ref, *, bq, bk):
    h = pl.program_id(0)
    qi = pl.program_id(1)
    bias2 = bias_ref[h] * LOG2E
    ntri = ntri_ref[...]
    n_sub = bq // bk
    row = lax.broadcasted_iota(jnp.int32, (bk, bk), 0)
    col = lax.broadcasted_iota(jnp.int32, (bk, bk), 1)
    allowed_top = col < row

    for c in reversed(range(n_sub)):
        r0 = c * bk
        k0 = pl.multiple_of(qi * bq + r0, bk)
        if c == n_sub - 1:
            carry_in = jnp.zeros((bk, 1), F32)
        else:
            carry_in = jnp.concatenate([jnp.zeros((bk, 1), F32), carry_ref[r0 + bk:, :]], axis=0)
        pv, carry = _sb_block(q_ref[r0:, :], k_ref[pl.ds(k0, bk), :], v_ref[pl.ds(k0, bk), :], ntri, bias2,
                              carry_in, allowed_top)
        acc_ref[r0:r0 + bk, :] = pv[:bk]
        if c < n_sub - 1:
            acc_ref[r0 + bk:, :] += pv[bk:]
        carry_ref[r0:, :] = carry

    def body(t, _):
        k0 = pl.multiple_of(qi * bq - (2 * t + 1) * bk, bk)
        k1 = pl.multiple_of(qi * bq - (2 * t + 2) * bk, bk)
        pv0, carry = _sb_block(q_ref[...], k_ref[pl.ds(k0, bk), :], v_ref[pl.ds(k0, bk), :], ntri, bias2,
                               carry_ref[...], None)
        pv1, carry = _sb_block(q_ref[...], k_ref[pl.ds(k1, bk), :], v_ref[pl.ds(k1, bk), :], ntri, bias2,
                               carry, None)
        acc_ref[...] += pv0 + pv1
        carry_ref[...] = carry
        return 0

    assert n_sub % 2 == 0
    lax.fori_loop(0, qi * (n_sub // 2), body, 0)
    o_ref[...] = acc_ref[...].astype(o_ref.dtype)


def _tri_ge(n, value=1.0):
    i = np.arange(n)
    return jnp.asarray((i[:, None] >= i[None, :]).astype(np.float32) * value, dtype=BF16)


def _sb_attention_prompt(q, k, v, bias, *, bq=1024, bk=256):
    t = q.shape[0]
    dh = SB_HEAD_DIM
    return pl.pallas_call(
        functools.partial(_sb_prompt_kernel, bq=bq, bk=bk),
        grid=(SB_HEADS, t // bq),
        in_specs=[
            pl.BlockSpec(memory_space=pltpu.SMEM),
            pl.BlockSpec((bq, dh), lambda h, i: (i, h)),
            pl.BlockSpec((t, dh), lambda h, i: (0, h)),
            pl.BlockSpec((t, dh), lambda h, i: (0, h)),
            pl.BlockSpec((bk, bk), lambda h, i: (0, 0)),
        ],
        out_specs=pl.BlockSpec((bq, dh), lambda h, i: (i, h)),
        out_shape=jax.ShapeDtypeStruct((t, SB_HEADS * dh), BF16),
        scratch_shapes=[pltpu.VMEM((bq, dh), F32), pltpu.VMEM((bq, 1), F32)],
        compiler_params=_cparams(2), name="sb_attn_prompt",
    )(bias.astype(F32), q, k, v, _tri_ge(bk, -1.0))


def _sb_sample_kernel(pt_ref, q_ref, k2_ref, v2_ref, ka_ref, va_ref, kb_ref, vb_ref, ntri_ref, bias_ref, o_ref,
                      acc_ref, carry_ref, *, n_q, page):
    s = pl.program_id(1)
    dh = SB_HEAD_DIM
    nrow = SB_HEADS * n_q
    pair = 2 * n_q

    q_rows = jnp.concatenate([q_ref[:, h * dh:(h + 1) * dh] for h in range(SB_HEADS)], axis=0).astype(BF16)

    def step(k_of_head, v_of_head, allowed):
        zs = []
        for h in range(SB_HEADS):
            g = h // 2
            zz = lax.dot_general(q_rows[g * pair:(g + 1) * pair], k_of_head(h), (((1,), (1,)), ((), ())),
                                 preferred_element_type=F32)
            zs.append(zz[(h % 2) * n_q:(h % 2 + 1) * n_q])
        z = jnp.concatenate(zs, axis=0) + bias_ref[...]
        sp = _softplus2(z)
        if allowed is not None:
            sp = jnp.where(allowed, sp, 0.0)
        cum = jnp.dot(sp.astype(BF16), ntri_ref[...], preferred_element_type=F32)
        w = jnp.exp2(z + cum + carry_ref[...])
        if allowed is not None:
            w = jnp.where(allowed, w, 0.0)
        carry_ref[...] += cum[:, 0:1]
        wb = w.astype(BF16)
        for h in range(SB_HEADS):
            g = h // 2
            pv = jnp.dot(wb[g * pair:(g + 1) * pair], v_of_head(h), preferred_element_type=F32)
            acc_ref[h * n_q:(h + 1) * n_q, :] += pv[(h % 2) * n_q:(h % 2 + 1) * n_q]

    def cache_heads(ref):
        return lambda h: ref[pl.ds(h, page, stride=SB_HEADS), :].astype(BF16)

    @pl.when(s == 0)
    def _():
        acc_ref[...] = jnp.zeros_like(acc_ref)
        carry_ref[...] = jnp.zeros_like(carry_ref)
        pad = jnp.zeros((page - n_q, dh), BF16)
        row = lax.broadcasted_iota(jnp.int32, (nrow, page), 0)
        col = lax.broadcasted_iota(jnp.int32, (nrow, page), 1)
        allowed = col < (row % n_q)
        step(lambda h: jnp.concatenate([k2_ref[:, h * dh:(h + 1) * dh].astype(BF16), pad], axis=0),
             lambda h: jnp.concatenate([v2_ref[:, h * dh:(h + 1) * dh].astype(BF16), pad], axis=0),
             allowed)

    @pl.when(s > 0)
    def _():
        step(cache_heads(ka_ref), cache_heads(va_ref), None)
        step(cache_heads(kb_ref), cache_heads(vb_ref), None)

    @pl.when(s == pl.num_programs(1) - 1)
    def _():
        for h in range(SB_HEADS):
            o_ref[:, h * dh:(h + 1) * dh] = acc_ref[h * n_q:(h + 1) * n_q, :]


def _sb_attention_sample(q, k2, v2, cache_k, cache_v, page_table, bias, *, n_q, page):
    nb, n_pages = page_table.shape
    assert n_pages % 2 == 0
    hd = SB_HEADS * SB_HEAD_DIM
    nrow = SB_HEADS * n_q
    prow = page * SB_HEADS

    def cache_map(first):
        def index_map(b, s, pt):
            return (pt[b, n_pages + (1 if first else 0) - 2 * jnp.maximum(s, 1)], 0, 0)
        return index_map

    bias_rows = jnp.broadcast_to(jnp.repeat(bias.astype(F32) * LOG2E, n_q)[:, None], (nrow, page))
    grid_spec = pltpu.PrefetchScalarGridSpec(
        num_scalar_prefetch=1,
        grid=(nb, n_pages // 2 + 1),
        in_specs=[
            pl.BlockSpec((n_q, hd), lambda b, s, pt: (b, 0)),
            pl.BlockSpec((n_q, hd), lambda b, s, pt: (b, 0)),
            pl.BlockSpec((n_q, hd), lambda b, s, pt: (b, 0)),
            pl.BlockSpec((None, prow, SB_HEAD_DIM), cache_map(True)),
            pl.BlockSpec((None, prow, SB_HEAD_DIM), cache_map(True)),
            pl.BlockSpec((None, prow, SB_HEAD_DIM), cache_map(False)),
            pl.BlockSpec((None, prow, SB_HEAD_DIM), cache_map(False)),
            pl.BlockSpec((page, page), lambda b, s, pt: (0, 0)),
            pl.BlockSpec((nrow, page), lambda b, s, pt: (0, 0)),
        ],
        out_specs=pl.BlockSpec((n_q, hd), lambda b, s, pt: (b, 0)),
        scratch_shapes=[pltpu.VMEM((nrow, SB_HEAD_DIM), F32), pltpu.VMEM((nrow, 1), F32)],
    )
    return pl.pallas_call(
        functools.partial(_sb_sample_kernel, n_q=n_q, page=page),
        grid_spec=grid_spec,
        out_shape=jax.ShapeDtypeStruct((nb * n_q, hd), F32),
        compiler_params=_cparams(2), name="sb_attn_sample",
    )(page_table, q, k2, v2, cache_k, cache_v, cache_k, cache_v, _tri_ge(page, -1.0), bias_rows)


def _gla_masks(r, c, sc):
    i = np.arange(r)
    chunk = i // c
    cstart = chunk * c
    sub0 = (i // sc) * sc
    same = chunk[:, None] == chunk[None, :]
    il, ll = i[:, None], i[None, :]
    nb = c // sc
    mats = [
        same & (ll <= il),
        (sub0[:, None] < ll) & (ll <= il),
        same & (il < ll),
    ]
    for blk in range(1, nb):
        mats.append(same & (il < ll) & (ll <= (cstart + sc * blk)[:, None]))
    sums = np.concatenate(mats, axis=0).astype(np.float32)
    sel = []
    for blk in range(1, nb):
        sel.append(same & (((i % c) // sc) == blk)[:, None] & (ll < (cstart + sc * blk)[:, None]))
    sel = np.stack(sel, axis=0).astype(np.float32) if sel else np.zeros((1, r, r), np.float32)
    return jnp.asarray(sums, dtype=BF16), jnp.asarray(sel, dtype=F32)


def _gla_prompt_kernel(qk_ref, v_ref, r_ref, lr_ref, wa_ref, ba_ref, gn_ref, sums_ref, sel_ref,
                       o_ref, s_out_ref, s_ref, g_scr, k_scr, a_scr, *, rows, chunk, sub):
    t = pl.program_id(0)
    nk = GLA_HEADS * GLA_DK
    n_chunks = rows // chunk
    n_blk = chunk // sub
    n_sub = rows // sub

    @pl.when(t == 0)
    def _():
        s_ref[...] = jnp.zeros_like(s_ref)

    lr = lr_ref[...].astype(BF16)
    for h in range(GLA_HEADS):
        ksl = slice(h * GLA_DK, (h + 1) * GLA_DK)
        vsl = slice(h * GLA_DV, (h + 1) * GLA_DV)
        x = jnp.dot(lr, wa_ref[:, ksl], preferred_element_type=F32) + ba_ref[:, ksl]
        g = _log2_gate(x)
        g_hi = g.astype(BF16)
        g_lo = (g - g_hi.astype(F32)).astype(BF16)
        e = (jnp.dot(sums_ref[...], g_hi, preferred_element_type=F32)
             + jnp.dot(sums_ref[...], g_lo, preferred_element_type=F32))
        gcum = e[0:rows]
        qs = qk_ref[:, ksl].astype(F32) * (GLA_DK ** -0.5)
        kf = qk_ref[:, nk + h * GLA_DK: nk + (h + 1) * GLA_DK].astype(F32)
        vb = v_ref[:, vsl]
        q_hat = (qs * jnp.exp2(gcum)).astype(BF16)
        q_til = (qs * jnp.exp2(e[rows:2 * rows])).astype(BF16)
        k_hat = (kf * jnp.exp2(e[2 * rows:3 * rows])).astype(BF16)

        a = jnp.zeros((rows, rows), F32)
        for b in range(1, n_blk):
            k_til = (kf * jnp.exp2(e[(2 + b) * rows:(3 + b) * rows])).astype(BF16)
            p = lax.dot_general(q_til, k_til, (((1,), (1,)), ((), ())), preferred_element_type=F32)
            a = a + p * sel_ref[b - 1]
        a_scr[...] = a

        g_scr[...] = gcum
        k_scr[...] = kf
        lane = lax.broadcasted_iota(jnp.int32, (sub, LANES), 1)
        rowi = lax.broadcasted_iota(jnp.int32, (sub, 1), 0)
        for sci in range(n_sub):
            r0 = sci * sub
            q_s = qs[r0:r0 + sub]
            g_s = gcum[r0:r0 + sub]
            lane0 = r0 % LANES
            tile = r0 // LANES
            blk_a = jnp.zeros((sub, LANES), F32)
            for j in range(sub):
                kj = k_scr[r0 + j:r0 + j + 1, :]
                gj = g_scr[r0 + j:r0 + j + 1, :]
                tt = q_s * kj * jnp.exp2(g_s - gj)
                colv = jnp.sum(tt, axis=-1, keepdims=True)
                colv = jnp.where(rowi >= j, colv, 0.0)
                blk_a = jnp.where(lane == lane0 + j, colv, blk_a)
            a_scr[r0:r0 + sub, tile * LANES:(tile + 1) * LANES] += blk_a

        o_intra = jnp.dot(a_scr[...].astype(BF16), vb, preferred_element_type=F32)

        gcum_t = gcum.T
        outs = []
        for c in range(n_chunks):
            c0 = c * chunk
            s_old = s_ref[h]
            o_c = o_intra[c0:c0 + chunk] + jnp.dot(q_hat[c0:c0 + chunk], s_old.astype(BF16),
                                                  preferred_element_type=F32)
            decay = jnp.exp2(gcum_t[:, c0 + chunk - 1:c0 + chunk])
            kv = lax.dot_general(k_hat[c0:c0 + chunk], vb[c0:c0 + chunk], (((0,), (0,)), ((), ())),
                                 preferred_element_type=F32)
            s_ref[h] = decay * s_old + kv
            outs.append(o_c)
        o_h = jnp.concatenate(outs, axis=0) if n_chunks > 1 else outs[0]
        o_n = o_h * _rms_scale(o_h) * gn_ref[...]
        r_h = r_ref[:, vsl].astype(F32)
        o_ref[:, vsl] = (o_n * (r_h / (1.0 + jnp.exp(-r_h)))).astype(o_ref.dtype)

    @pl.when(t == pl.num_programs(0) - 1)
    def _():
        s_out_ref[...] = s_ref[...]


def _gla_prompt(p, lr, w_a2p, b_a, g_norm, *, rows=256):
    t = p.shape[0]
    nv = GLA_HEADS * GLA_DV
    sums, sel = _gla_masks(rows, GLA_CHUNK, GLA_SUB)
    n_sums = sums.shape[0]
    o, s = pl.pallas_call(
        functools.partial(_gla_prompt_kernel, rows=rows, chunk=GLA_CHUNK, sub=GLA_SUB),
        grid=(t // rows,),
        in_specs=[
            pl.BlockSpec((rows, nv), lambda i: (i, 0)),
            pl.BlockSpec((rows, nv), lambda i: (i, 1)),
            pl.BlockSpec((rows, nv), lambda i: (i, 2)),
            pl.BlockSpec((rows, LANES), lambda i: (i, 0)),
            pl.BlockSpec((LANES, GLA_HEADS * GLA_DK), lambda i: (0, 0)),
            pl.BlockSpec((1, GLA_HEADS * GLA_DK), lambda i: (0, 0)),
            pl.BlockSpec((1, GLA_DV), lambda i: (0, 0)),
            pl.BlockSpec((n_sums, rows), lambda i: (0, 0)),
            pl.BlockSpec(sel.shape, lambda i: (0, 0, 0)),
        ],
        out_specs=[
            pl.BlockSpec((rows, nv), lambda i: (i, 0)),
            pl.BlockSpec((GLA_HEADS, GLA_DK, GLA_DV), lambda i: (0, 0, 0)),
        ],
        out_shape=[
            jax.ShapeDtypeStruct((t, nv), BF16),
            jax.ShapeDtypeStruct((GLA_HEADS, GLA_DK, GLA_DV), F32),
        ],
        scratch_shapes=[
            pltpu.VMEM((GLA_HEADS, GLA_DK, GLA_DV), F32),
            pltpu.VMEM((rows, GLA_DK), F32),
            pltpu.VMEM((rows, GLA_DK), F32),
            pltpu.VMEM((rows, rows), F32),
        ],
        compiler_params=_cparams(1), name="gla_prompt",
    )(p, p, p, lr, w_a2p, b_a.reshape(1, -1).astype(F32), g_norm.reshape(1, -1).astype(F32), sums, sel)
    return o, s


def _gla_sample_kernel(q_ref, k_ref, v_ref, r_ref, lr_ref, wa_ref, ba_ref, gn_ref, s0_ref, o_ref, s_out_ref,
                       *, n_t):
    pad = jnp.zeros((n_t, LANES), F32)
    lr16 = jnp.concatenate([lr_ref[...], pad], axis=0).astype(BF16)
    x = jnp.dot(lr16, wa_ref[...], preferred_element_type=F32)[0:n_t] + ba_ref[...]
    g = _log2_gate(x)
    rowi = lax.broadcasted_iota(jnp.int32, (n_t, 1), 0)
    gcum = jnp.zeros_like(g)
    for l in range(n_t):
        gcum = gcum + jnp.where(rowi >= l, g[l:l + 1, :], 0.0)
    qs = q_ref[...].astype(F32) * (GLA_DK ** -0.5)
    kf = k_ref[...].astype(F32)
    vf = v_ref[...].astype(F32)
    s_old = s0_ref[...]

    o = jnp.zeros((n_t, GLA_DV), F32)
    for j in range(n_t):
        tt = qs * kf[j:j + 1, :] * jnp.exp2(gcum - gcum[j:j + 1, :])
        colv = jnp.where(rowi >= j, jnp.sum(tt, axis=-1, keepdims=True), 0.0)
        o = o + colv * vf[j:j + 1, :]

    zpad = jnp.zeros((n_t, GLA_DK), F32)
    q_hat = jnp.concatenate([qs * jnp.exp2(gcum), zpad], axis=0).astype(BF16)
    o = o + jnp.dot(q_hat, s_old.astype(BF16), preferred_element_type=F32)[0:n_t]

    g_last = gcum[n_t - 1:n_t, :]
    k_hat = jnp.concatenate([kf * jnp.exp2(g_last - gcum), zpad], axis=0).astype(BF16)
    v16 = jnp.concatenate([vf, jnp.zeros((n_t, GLA_DV), F32)], axis=0).astype(BF16)
    kv = lax.dot_general(k_hat, v16, (((0,), (0,)), ((), ())), preferred_element_type=F32)
    decay = jnp.exp2(jnp.broadcast_to(g_last, (LANES, GLA_DK)).T[:, 0:1])
    s_out_ref[...] = decay * s_old + kv

    o_n = o * _rms_scale(o) * gn_ref[...]
    r_h = r_ref[...].astype(F32)
    o_ref[...] = o_n * (r_h / (1.0 + jnp.exp(-r_h)))


def _gla_sample(p, lr, w_a2p, b_a, g_norm, s0, *, n_t, n_seq):
    nb = n_seq
    nh = GLA_HEADS
    o, s = pl.pallas_call(
        functools.partial(_gla_sample_kernel, n_t=n_t),
        grid=(nb, GLA_HEADS),
        in_specs=[
            pl.BlockSpec((n_t, GLA_DK), lambda b, h: (b, h)),
            pl.BlockSpec((n_t, GLA_DK), lambda b, h: (b, nh + h)),
            pl.BlockSpec((n_t, GLA_DV), lambda b, h: (b, nh + h)),
            pl.BlockSpec((n_t, GLA_DV), lambda b, h: (b, 2 * nh + h)),
            pl.BlockSpec((n_t, LANES), lambda b, h: (b, 0)),
            pl.BlockSpec((LANES, GLA_DK), lambda b, h: (0, h)),
            pl.BlockSpec((1, GLA_DK), lambda b, h: (0, h)),
            pl.BlockSpec((1, GLA_DV), lambda b, h: (0, 0)),
            pl.BlockSpec((None, None, GLA_DK, GLA_DV), lambda b, h: (b, h, 0, 0)),
        ],
        out_specs=[
            pl.BlockSpec((n_t, GLA_DV), lambda b, h: (b, h)),
            pl.BlockSpec((None, None, GLA_DK, GLA_DV), lambda b, h: (b, h, 0, 0)),
        ],
        out_shape=[
            jax.ShapeDtypeStruct((nb * n_t, GLA_HEADS * GLA_DV), F32),
            jax.ShapeDtypeStruct((nb,) + s0.shape[1:], F32),
        ],
        compiler_params=_cparams(2), name="gla_sample",
    )(p, p, p, p, lr, w_a2p, b_a.reshape(1, -1).astype(F32), g_norm.reshape(1, -1).astype(F32), s0)
    return o, s


def kernel(x_prompt, x_sample, cache_k, cache_v, page_table, state_gla, norm_mix, sb_w_qkv, sb_bias, sb_w_o,
           gla_w_in, gla_w_a2, gla_b_a, gla_g_norm, gla_w_o, norm_ffn, w_up, w_down, norm_final):
    bp, tp, d = x_prompt.shape
    bs, ts, _ = x_sample.shape
    assert bp == 1
    hd = SB_HEADS * SB_HEAD_DIM
    xp = x_prompt.reshape(bp * tp, d)
    xs = x_sample.reshape(bs * ts, d)
    scale = SB_HEAD_DIM ** -0.5 * LOG2E
    nk = GLA_HEADS * GLA_DK
    nv = GLA_HEADS * GLA_DV

    wqkv = sb_w_qkv.astype(BF16)
    wo = sb_w_o.astype(BF16)
    g0 = norm_mix[0]

    q_p, k_p, k_pb, v_p, v_pb = _proj(
        xp, wqkv, layer=0, gain=g0, name="sb_qkv_prompt",
        outs=[(BF16, 0, hd, scale), (F32, hd, hd, None), (BF16, hd, hd, None),
              (F32, 2 * hd, hd, None), (BF16, 2 * hd, hd, None)])
    o_p = _sb_attention_prompt(q_p, k_pb, v_pb, sb_bias[0])
    (xp,) = _proj(o_p, wo, layer=0, res=xp, name="sb_o_prompt")

    q_s, k_s, v_s = _proj(xs, wqkv, layer=0, gain=g0, name="sb_qkv_sample",
                          outs=[(F32, 0, hd, scale), (F32, hd, hd, None), (F32, 2 * hd, hd, None)])
    page = cache_k.shape[2]
    o_s = _sb_attention_sample(q_s, k_s, v_s, cache_k.reshape(-1, page * SB_HEADS, SB_HEAD_DIM),
                               cache_v.reshape(-1, page * SB_HEADS, SB_HEAD_DIM), page_table, sb_bias[0],
                               n_q=ts, page=page)
    (xs,) = _proj(o_s, wo, layer=0, res=xs, name="sb_o_sample")

    wu, wd = w_up.astype(BF16), w_down.astype(BF16)
    xp = _mlp(xp, norm_ffn[0], wu, wd, 0, norm_final, final_norm=False, name="mlp0_prompt")
    xs = _mlp(xs, norm_ffn[0], wu, wd, 0, norm_final, final_norm=False, name="mlp0_sample")

    w_in = gla_w_in[0]
    w_main = jnp.concatenate([w_in[:, :2 * nk + nv], w_in[:, 2 * nk + nv + GLA_GATE_RANK:]], axis=1).astype(BF16)
    w_lr = jnp.pad(w_in[:, 2 * nk + nv:2 * nk + nv + GLA_GATE_RANK],
                   ((0, 0), (0, LANES - GLA_GATE_RANK))).astype(BF16)
    w_a2p = jnp.pad(gla_w_a2[0], ((0, LANES - GLA_GATE_RANK), (0, 0))).astype(BF16)
    w_go = gla_w_o.astype(BF16)
    g1 = norm_mix[1]
    n_main = w_main.shape[1]

    (p_p,) = _proj(xp, w_main, gain=g1, outs=[(BF16, 0, n_main, None)], name="gla_in_prompt")
    (lr_p,) = _proj(xp, w_lr, gain=g1, name="gla_lr_prompt")
    go_p, s_p = _gla_prompt(p_p, lr_p, w_a2p, gla_b_a[0], gla_g_norm[0])
    (xp,) = _proj(go_p, w_go, layer=0, res=xp, name="gla_o_prompt")

    (p_s,) = _proj(xs, w_main, gain=g1, name="gla_in_sample")
    (lr_s,) = _proj(xs, w_lr, gain=g1, name="gla_lr_sample")
    s0 = state_gla.reshape(-1, GLA_HEADS, GLA_DK, GLA_DV)
    go_s, s_s = _gla_sample(p_s, lr_s, w_a2p, gla_b_a[0], gla_g_norm[0], s0, n_t=ts, n_seq=bs)
    (xs,) = _proj(go_s, w_go, layer=0, res=xs, name="gla_o_sample")

    y_p = _mlp(xp, norm_ffn[1], wu, wd, 1, norm_final, final_norm=True, name="mlp1_prompt")
    y_s = _mlp(xs, norm_ffn[1], wu, wd, 1, norm_final, final_norm=True, name="mlp1_sample")

    return (
        y_p.reshape(bp, tp, d),
        y_s.reshape(bs, ts, d),
        k_p.reshape(1, bp, tp, SB_HEADS, SB_HEAD_DIM),
        v_p.reshape(1, bp, tp, SB_HEADS, SB_HEAD_DIM),
        k_s.reshape(1, bs, ts, SB_HEADS, SB_HEAD_DIM),
        v_s.reshape(1, bs, ts, SB_HEADS, SB_HEAD_DIM),
        s_p.reshape(1, bp, GLA_HEADS, GLA_DK, GLA_DV),
        s_s.reshape(1, bs, GLA_HEADS, GLA_DK, GLA_DV),
    )
```

```python
import functools

import numpy as np
import jax
import jax.numpy as jnp
from jax import lax
from jax.experimental import pallas as pl
from jax.experimental.pallas import tpu as pltpu

F32 = jnp.float32
BF16 = jnp.bfloat16

EPS = 1e-6
SB_HEADS = 16
SB_HEAD_DIM = 128
GLA_HEADS = 4
GLA_DK = 256
GLA_DV = 512
GLA_GATE_RANK = 16
GLA_TAU = 16.0
GLA_CHUNK = 64
GLA_SUB = 16
SAMPLE_PAGES_PER_STEP = 4
LANES = 128
LOG2E = 1.4426950408889634
VMEM_LIMIT_BYTES = 56 * 1024 * 1024


def _cparams(n_grid_dims):
    return pltpu.CompilerParams(
        dimension_semantics=("arbitrary",) * n_grid_dims,
        vmem_limit_bytes=VMEM_LIMIT_BYTES)


def _log2_gate(x):
    softplus_neg = jnp.maximum(-x, 0.0) + jnp.log(1.0 + jnp.exp(-jnp.abs(x)))
    return softplus_neg * (-LOG2E / GLA_TAU)


def _rms_scale(x):
    return lax.rsqrt(jnp.mean(x * x, axis=-1, keepdims=True) + EPS)


def _proj_kernel(*refs, has_norm, has_res, sections, n_tiles):
    it = iter(refs)
    x_ref = next(it)
    g_ref = next(it) if has_norm else None
    w_ref = next(it)
    res_ref = next(it) if has_res else None
    out_refs = [next(it) for _ in sections]
    h_ref = next(it) if has_norm else None
    j = pl.program_id(1)

    if has_norm:
        @pl.when(j == 0)
        def _():
            x = x_ref[...]
            h_ref[...] = (x * _rms_scale(x) * g_ref[...]).astype(BF16)
        h = h_ref[...]
    else:
        h = x_ref[...].astype(BF16)
    y = jnp.dot(h, w_ref[...], preferred_element_type=F32)
    if has_res:
        y = y + res_ref[...]
    for (first, count, scale), o_ref in zip(sections, out_refs):
        def write(o_ref=o_ref, scale=scale):
            o_ref[...] = (y if scale is None else y * scale).astype(o_ref.dtype)
        if first == 0 and count == n_tiles:
            write()
        else:
            pl.when((j >= first) & (j < first + count))(write)


def _proj(x, w, *, gain=None, res=None, outs=None, layer=None, tm=512, tn=1024, name="proj"):
    m, kdim = x.shape
    n = w.shape[-1]
    tm = min(tm, m)
    tn = min(tn, n)
    assert m % tm == 0 and n % tn == 0
    n_tiles = n // tn
    outs = outs or [(F32, 0, n, None)]
    has_norm = gain is not None
    has_res = res is not None
    assert not has_res or (len(outs) == 1 and outs[0][1] == 0 and outs[0][2] == n)
    in_specs = [pl.BlockSpec((tm, kdim), lambda i, j: (i, 0))]
    args = [x]
    if has_norm:
        in_specs.append(pl.BlockSpec((1, kdim), lambda i, j: (0, 0)))
        args.append(gain.reshape(1, kdim).astype(F32))
    if layer is None:
        in_specs.append(pl.BlockSpec((kdim, tn), lambda i, j: (0, j)))
    else:
        in_specs.append(pl.BlockSpec((None, kdim, tn), lambda i, j: (layer, 0, j)))
    args.append(w)
    if has_res:
        in_specs.append(pl.BlockSpec((tm, tn), lambda i, j: (i, j)))
        args.append(res)
    sections, out_shape, out_specs = [], [], []
    for dt, col0, cols, scale in outs:
        assert col0 % tn == 0 and cols % tn == 0
        first, count = col0 // tn, cols // tn
        sections.append((first, count, scale))
        out_shape.append(jax.ShapeDtypeStruct((m, cols), dt))
        out_specs.append(pl.BlockSpec(
            (tm, tn), lambda i, j, first=first, count=count: (i, jnp.clip(j - first, 0, count - 1))))
    scratch = [pltpu.VMEM((tm, kdim), BF16)] if has_norm else []
    return pl.pallas_call(
        functools.partial(_proj_kernel, has_norm=has_norm, has_res=has_res,
                          sections=tuple(sections), n_tiles=n_tiles),
        grid=(m // tm, n_tiles),
        in_specs=in_specs, out_specs=out_specs, out_shape=out_shape,
        scratch_shapes=scratch, compiler_params=_cparams(2), name=name,
    )(*args)


def _mlp_kernel(x_ref, g_ref, wu_ref, wd_ref, gf_ref, o_ref, h_ref, *, final_norm):
    c = pl.program_id(1)

    @pl.when(c == 0)
    def _():
        x = x_ref[...]
        h_ref[...] = (x * _rms_scale(x) * g_ref[...]).astype(BF16)
        o_ref[...] = x

    u = jnp.maximum(jnp.dot(h_ref[...], wu_ref[...], preferred_element_type=F32), 0.0)
    o_ref[...] += jnp.dot((u * u).astype(BF16), wd_ref[...], preferred_element_type=F32)

    if final_norm:
        @pl.when(c == pl.num_programs(1) - 1)
        def _():
            y = o_ref[...]
            o_ref[...] = y * _rms_scale(y) * gf_ref[...]


def _mlp(x, gain, w_up, w_down, layer, gain_final, *, final_norm, tm=512, tf=1024, name="mlp"):
    m, d = x.shape
    dff = w_up.shape[2]
    tm = min(tm, m)
    assert m % tm == 0 and dff % tf == 0
    return pl.pallas_call(
        functools.partial(_mlp_kernel, final_norm=final_norm),
        grid=(m // tm, dff // tf),
        in_specs=[
            pl.BlockSpec((tm, d), lambda i, c: (i, 0)),
            pl.BlockSpec((1, d), lambda i, c: (0, 0)),
            pl.BlockSpec((None, d, tf), lambda i, c: (layer, 0, c)),
            pl.BlockSpec((None, tf, d), lambda i, c: (layer, c, 0)),
            pl.BlockSpec((1, d), lambda i, c: (0, 0)),
        ],
        out_specs=pl.BlockSpec((tm, d), lambda i, c: (i, 0)),
        out_shape=jax.ShapeDtypeStruct((m, d), F32),
        scratch_shapes=[pltpu.VMEM((tm, d), BF16)],
        compiler_params=_cparams(2), name=name,
    )(x, gain.reshape(1, d).astype(F32), w_up, w_down, gain_final.reshape(1, d).astype(F32))


def _softplus2(z2):
    neg_abs = lax.bitcast_convert_type(lax.bitcast_convert_type(z2, jnp.uint32) | jnp.uint32(0x80000000), F32)
    return jnp.maximum(z2, 0.0) + jnp.log2(1.0 + jnp.exp2(neg_abs))


def _sb_block(q, kb, vb, ntri, bias2, carry, allowed_top):
    def mask_top(x):
        if allowed_top is None:
            return x
        nt = allowed_top.shape[0]
        top = jnp.where(allowed_top, x[:nt], 0.0)
        return top if x.shape[0] == nt else jnp.concatenate([top, x[nt:]], axis=0)

    z = lax.dot_general(q, kb, (((1,), (1,)), ((), ())), preferred_element_type=F32) + bias2
    sp = mask_top(_softplus2(z))
    cum = jnp.dot(sp.astype(BF16), ntri, preferred_element_type=F32)
    w = mask_top(jnp.exp2((z + carry) + cum))
    pv = jnp.dot(w.astype(BF16), vb, preferred_element_type=F32)
    return pv, carry + cum[:, 0:1]


def _sb_prompt_kernel(bias_ref, q_ref, k_ref, v_ref, ntri_ref, o_ref, acc_ref, carry_ref, *, bq, bk, unroll):
    h = pl.program_id(0)
    qi = pl.program_id(1)
    bias2 = bias_ref[h] * LOG2E
    ntri = ntri_ref[...]
    n_sub = bq // bk
    row = lax.broadcasted_iota(jnp.int32, (bk, bk), 0)
    col = lax.broadcasted_iota(jnp.int32, (bk, bk), 1)
    allowed_top = col < row

    for c in reversed(range(n_sub)):
        r0 = c * bk
        k0 = pl.multiple_of(qi * bq + r0, bk)
        if c == n_sub - 1:
            carry_in = jnp.zeros((bk, 1), F32)
        else:
            carry_in = jnp.concatenate([jnp.zeros((bk, 1), F32), carry_ref[r0 + bk:, :]], axis=0)
        pv, carry = _sb_block(q_ref[r0:, :], k_ref[pl.ds(k0, bk), :], v_ref[pl.ds(k0, bk), :], ntri, bias2,
                              carry_in, allowed_top)
        acc_ref[r0:r0 + bk, :] = pv[:bk]
        if c < n_sub - 1:
            acc_ref[r0 + bk:, :] += pv[bk:]
        carry_ref[r0:, :] = carry

    def body(t, _):
        carry = carry_ref[...]
        pv_sum = None
        for u in range(unroll):
            k0 = pl.multiple_of(qi * bq - (unroll * t + u + 1) * bk, bk)
            pv, carry = _sb_block(q_ref[...], k_ref[pl.ds(k0, bk), :], v_ref[pl.ds(k0, bk), :], ntri, bias2,
                                  carry, None)
            pv_sum = pv if pv_sum is None else pv_sum + pv
        acc_ref[...] += pv_sum
        carry_ref[...] = carry
        return 0

    assert n_sub % unroll == 0
    lax.fori_loop(0, qi * (n_sub // unroll), body, 0)
    o_ref[...] = acc_ref[...].astype(o_ref.dtype)


def _tri_ge(n, value=1.0):
    i = np.arange(n)
    return jnp.asarray((i[:, None] >= i[None, :]).astype(np.float32) * value, dtype=BF16)


def _sb_attention_prompt(q, k, v, bias, *, bq=1024, bk=256, unroll=4):
    t = q.shape[0]
    dh = SB_HEAD_DIM
    return pl.pallas_call(
        functools.partial(_sb_prompt_kernel, bq=bq, bk=bk, unroll=unroll),
        grid=(SB_HEADS, t // bq),
        in_specs=[
            pl.BlockSpec(memory_space=pltpu.SMEM),
            pl.BlockSpec((bq, dh), lambda h, i: (i, h)),
            pl.BlockSpec((t, dh), lambda h, i: (0, h)),
            pl.BlockSpec((t, dh), lambda h, i: (0, h)),
            pl.BlockSpec((bk, bk), lambda h, i: (0, 0)),
        ],
        out_specs=pl.BlockSpec((bq, dh), lambda h, i: (i, h)),
        out_shape=jax.ShapeDtypeStruct((t, SB_HEADS * dh), BF16),
        scratch_shapes=[pltpu.VMEM((bq, dh), F32), pltpu.VMEM((bq, 1), F32)],
        compiler_params=_cparams(2), name="sb_attn_prompt",
    )(bias.astype(F32), q, k, v, _tri_ge(bk, -1.0))


def _sb_sample_kernel(pt_ref, q_ref, k2_ref, v2_ref, *rest, n_q, page, pps):
    k_refs, v_refs = rest[:pps], rest[pps:2 * pps]
    ntri_ref, npar_ref, bias_ref, o_ref, acc_ref, carry_ref, kpk_ref, vpk_ref = rest[2 * pps:]
    s = pl.program_id(1)
    dh = SB_HEAD_DIM
    n_pairs = SB_HEADS // 2
    prow = 2 * n_q
    nrow = SB_HEADS * n_q
    width = 2 * page

    q_rows = jnp.concatenate([q_ref[:, h * dh:(h + 1) * dh] for h in range(SB_HEADS)], axis=0).astype(BF16)
    row = lax.broadcasted_iota(jnp.int32, (nrow, width), 0)
    col = lax.broadcasted_iota(jnp.int32, (nrow, width), 1)
    head_parity = (row // n_q) % 2

    def step(k_of_pair, v_of_pair, allowed, cum_ref, total_cols):
        z = jnp.concatenate(
            [lax.dot_general(q_rows[a * prow:(a + 1) * prow], k_of_pair(a), (((1,), (1,)), ((), ())),
                             preferred_element_type=F32) for a in range(n_pairs)], axis=0) + bias_ref[...]
        sp = jnp.where(allowed, _softplus2(z), 0.0)
        cum = jnp.dot(sp.astype(BF16), cum_ref[...], preferred_element_type=F32)
        w = jnp.where(allowed, jnp.exp2(z + cum + carry_ref[...]), 0.0)
        total = cum[:, 0:1]
        for c in range(1, total_cols):
            total = total + cum[:, c:c + 1]
        carry_ref[...] += total
        wb = w.astype(BF16)
        for a in range(n_pairs):
            acc_ref[a * prow:(a + 1) * prow, :] += jnp.dot(wb[a * prow:(a + 1) * prow], v_of_pair(a),
                                                           preferred_element_type=F32)

    @pl.when(s == 0)
    def _():
        acc_ref[...] = jnp.zeros_like(acc_ref)
        carry_ref[...] = jnp.zeros_like(carry_ref)
        pad = jnp.zeros((width - prow, dh), F32)

        def new_pair(ref):
            return lambda a: jnp.concatenate([ref[:, 2 * a * dh:(2 * a + 1) * dh],
                                              ref[:, (2 * a + 1) * dh:(2 * a + 2) * dh], pad], axis=0).astype(BF16)

        allowed = (col // n_q == head_parity) & (col % n_q < row % n_q)
        step(new_pair(k2_ref), new_pair(v2_ref), allowed, ntri_ref, 1)

    @pl.when(s > 0)
    def _():
        allowed = col % 2 == head_parity
        for p in range(pps):
            kpk_ref[p] = pltpu.bitcast(k_refs[p][...].astype(BF16), jnp.uint32)
            vpk_ref[p] = pltpu.bitcast(v_refs[p][...].astype(BF16), jnp.uint32)

            def cache_pair(ref, p=p):
                return lambda a: pltpu.bitcast(ref[p, pl.ds(a, page, stride=n_pairs), :], BF16)

            step(cache_pair(kpk_ref), cache_pair(vpk_ref), allowed, npar_ref, 2)

    @pl.when(s == pl.num_programs(1) - 1)
    def _():
        for h in range(SB_HEADS):
            o_ref[:, h * dh:(h + 1) * dh] = acc_ref[h * n_q:(h + 1) * n_q, :]


def _sb_attention_sample(q, k2, v2, cache_k, cache_v, page_table, bias, *, n_q, page):
    nb, n_pages = page_table.shape
    pps = SAMPLE_PAGES_PER_STEP
    assert n_pages % pps == 0 and SB_HEADS % 2 == 0 and 2 * n_q <= 2 * page
    hd = SB_HEADS * SB_HEAD_DIM
    nrow = SB_HEADS * n_q
    prow = page * SB_HEADS
    width = 2 * page

    def cache_map(p):
        def index_map(b, s, pt):
            return (pt[b, n_pages - pps * (jnp.maximum(s, 1) - 1) - 1 - p], 0, 0)
        return index_map

    i = np.arange(width)
    ge = i[:, None] >= i[None, :]
    same_parity = (i[:, None] - i[None, :]) % 2 == 0
    ntri = jnp.asarray(-(ge).astype(np.float32), dtype=BF16)
    npar = jnp.asarray(-(ge & same_parity).astype(np.float32), dtype=BF16)
    bias_rows = jnp.broadcast_to(jnp.repeat(bias.astype(F32) * LOG2E, n_q)[:, None], (nrow, width))
    small = pl.BlockSpec((n_q, hd), lambda b, s, pt: (b, 0))
    const = lambda shape: pl.BlockSpec(shape, lambda b, s, pt: (0, 0))
    grid_spec = pltpu.PrefetchScalarGridSpec(
        num_scalar_prefetch=1,
        grid=(nb, n_pages // pps + 1),
        in_specs=([small, small, small]
                  + [pl.BlockSpec((None, prow, SB_HEAD_DIM), cache_map(p)) for p in range(pps)] * 2
                  + [const((width, width)), const((width, width)), const((nrow, width))]),
        out_specs=small,
        scratch_shapes=[pltpu.VMEM((nrow, SB_HEAD_DIM), F32), pltpu.VMEM((nrow, 1), F32),
                        pltpu.VMEM((pps, prow // 2, SB_HEAD_DIM), jnp.uint32),
                        pltpu.VMEM((pps, prow // 2, SB_HEAD_DIM), jnp.uint32)],
    )
    return pl.pallas_call(
        functools.partial(_sb_sample_kernel, n_q=n_q, page=page, pps=pps),
        grid_spec=grid_spec,
        out_shape=jax.ShapeDtypeStruct((nb * n_q, hd), F32),
        compiler_params=_cparams(2), name="sb_attn_sample",
    )(page_table, q, k2, v2, *([cache_k] * pps), *([cache_v] * pps), ntri, npar, bias_rows)


def _gla_masks(r, c, sc):
    i = np.arange(r)
    chunk = i // c
    cstart = chunk * c
    sub0 = (i // sc) * sc
    same = chunk[:, None] == chunk[None, :]
    il, ll = i[:, None], i[None, :]
    nb = c // sc
    mats = [
        same & (ll <= il),
        (sub0[:, None] < ll) & (ll <= il),
        same & (il < ll),
    ]
    for blk in range(1, nb):
        mats.append(same & (il < ll) & (ll <= (cstart + sc * blk)[:, None]))
    sums = np.concatenate(mats, axis=0).astype(np.float32)
    sel = []
    for blk in range(1, nb):
        sel.append(same & (((i % c) // sc) == blk)[:, None] & (ll < (cstart + sc * blk)[:, None]))
    sel = np.stack(sel, axis=0).astype(np.float32) if sel else np.zeros((1, r, r), np.float32)
    return jnp.asarray(sums, dtype=BF16), jnp.asarray(sel, dtype=F32)


def _gla_prompt_kernel(qk_ref, v_ref, r_ref, lr_ref, wa_ref, ba_ref, gn_ref, sums_ref, sel_ref,
                       o_ref, s_out_ref, s_ref, g_scr, k_scr, a_scr, *, rows, chunk, sub):
    t = pl.program_id(0)
    nk = GLA_HEADS * GLA_DK
    n_chunks = rows // chunk
    n_blk = chunk // sub
    n_sub = rows // sub

    @pl.when(t == 0)
    def _():
        s_ref[...] = jnp.zeros_like(s_ref)

    lr = lr_ref[...].astype(BF16)
    for h in range(GLA_HEADS):
        ksl = slice(h * GLA_DK, (h + 1) * GLA_DK)
        vsl = slice(h * GLA_DV, (h + 1) * GLA_DV)
        x = jnp.dot(lr, wa_ref[:, ksl], preferred_element_type=F32) + ba_ref[:, ksl]
        g = _log2_gate(x)
        g_hi = g.astype(BF16)
        g_lo = (g - g_hi.astype(F32)).astype(BF16)
        e = (jnp.dot(sums_ref[...], g_hi, preferred_element_type=F32)
             + jnp.dot(sums_ref[...], g_lo, preferred_element_type=F32))
        gcum = e[0:rows]
        qs = qk_ref[:, ksl].astype(F32) * (GLA_DK ** -0.5)
        kf = qk_ref[:, nk + h * GLA_DK: nk + (h + 1) * GLA_DK].astype(F32)
        vb = v_ref[:, vsl]
        q_hat = (qs * jnp.exp2(gcum)).astype(BF16)
        q_til = (qs * jnp.exp2(e[rows:2 * rows])).astype(BF16)
        k_hat = (kf * jnp.exp2(e[2 * rows:3 * rows])).astype(BF16)

        a = jnp.zeros((rows, rows), F32)
        for b in range(1, n_blk):
            k_til = (kf * jnp.exp2(e[(2 + b) * rows:(3 + b) * rows])).astype(BF16)
            p = lax.dot_general(q_til, k_til, (((1,), (1,)), ((), ())), preferred_element_type=F32)
            a = a + p * sel_ref[b - 1]
        a_scr[...] = a

        g_scr[...] = gcum
        k_scr[...] = kf
        lane = lax.broadcasted_iota(jnp.int32, (sub, LANES), 1)
        rowi = lax.broadcasted_iota(jnp.int32, (sub, 1), 0)
        for sci in range(n_sub):
            r0 = sci * sub
            q_s = qs[r0:r0 + sub]
            g_s = gcum[r0:r0 + sub]
            lane0 = r0 % LANES
            tile = r0 // LANES
            blk_a = jnp.zeros((sub, LANES), F32)
            for j in range(sub):
                kj = k_scr[r0 + j:r0 + j + 1, :]
                gj = g_scr[r0 + j:r0 + j + 1, :]
                tt = q_s * kj * jnp.exp2(g_s - gj)
                colv = jnp.sum(tt, axis=-1, keepdims=True)
                colv = jnp.where(rowi >= j, colv, 0.0)
                blk_a = jnp.where(lane == lane0 + j, colv, blk_a)
            a_scr[r0:r0 + sub, tile * LANES:(tile + 1) * LANES] += blk_a

        o_intra = jnp.dot(a_scr[...].astype(BF16), vb, preferred_element_type=F32)

        gcum_t = gcum.T
        outs = []
        for c in range(n_chunks):
            c0 = c * chunk
            s_old = s_ref[h]
            o_c = o_intra[c0:c0 + chunk] + jnp.dot(q_hat[c0:c0 + chunk], s_old.astype(BF16),
                                                  preferred_element_type=F32)
            decay = jnp.exp2(gcum_t[:, c0 + chunk - 1:c0 + chunk])
            kv = lax.dot_general(k_hat[c0:c0 + chunk], vb[c0:c0 + chunk], (((0,), (0,)), ((), ())),
                                 preferred_element_type=F32)
            s_ref[h] = decay * s_old + kv
            outs.append(o_c)
        o_h = jnp.concatenate(outs, axis=0) if n_chunks > 1 else outs[0]
        o_n = o_h * _rms_scale(o_h) * gn_ref[...]
        r_h = r_ref[:, vsl].astype(F32)
        o_ref[:, vsl] = (o_n * (r_h / (1.0 + jnp.exp(-r_h)))).astype(o_ref.dtype)

    @pl.when(t == pl.num_programs(0) - 1)
    def _():
        s_out_ref[...] = s_ref[...]


def _gla_prompt(p, lr, w_a2p, b_a, g_norm, *, rows=256):
    t = p.shape[0]
    nv = GLA_HEADS * GLA_DV
    sums, sel = _gla_masks(rows, GLA_CHUNK, GLA_SUB)
    n_sums = sums.shape[0]
    o, s = pl.pallas_call(
        functools.partial(_gla_prompt_kernel, rows=rows, chunk=GLA_CHUNK, sub=GLA_SUB),
        grid=(t // rows,),
        in_specs=[
            pl.BlockSpec((rows, nv), lambda i: (i, 0)),
            pl.BlockSpec((rows, nv), lambda i: (i, 1)),
            pl.BlockSpec((rows, nv), lambda i: (i, 2)),
            pl.BlockSpec((rows, LANES), lambda i: (i, 0)),
            pl.BlockSpec((LANES, GLA_HEADS * GLA_DK), lambda i: (0, 0)),
            pl.BlockSpec((1, GLA_HEADS * GLA_DK), lambda i: (0, 0)),
            pl.BlockSpec((1, GLA_DV), lambda i: (0, 0)),
            pl.BlockSpec((n_sums, rows), lambda i: (0, 0)),
            pl.BlockSpec(sel.shape, lambda i: (0, 0, 0)),
        ],
        out_specs=[
            pl.BlockSpec((rows, nv), lambda i: (i, 0)),
            pl.BlockSpec((GLA_HEADS, GLA_DK, GLA_DV), lambda i: (0, 0, 0)),
        ],
        out_shape=[
            jax.ShapeDtypeStruct((t, nv), BF16),
            jax.ShapeDtypeStruct((GLA_HEADS, GLA_DK, GLA_DV), F32),
        ],
        scratch_shapes=[
            pltpu.VMEM((GLA_HEADS, GLA_DK, GLA_DV), F32),
            pltpu.VMEM((rows, GLA_DK), F32),
            pltpu.VMEM((rows, GLA_DK), F32),
            pltpu.VMEM((rows, rows), F32),
        ],
        compiler_params=_cparams(1), name="gla_prompt",
    )(p, p, p, lr, w_a2p, b_a.reshape(1, -1).astype(F32), g_norm.reshape(1, -1).astype(F32), sums, sel)
    return o, s


def _gla_sample_kernel(q_ref, k_ref, v_ref, r_ref, lr_ref, wa_ref, ba_ref, gn_ref, s0_ref, o_ref, s_out_ref,
                       *, n_t):
    pad = jnp.zeros((n_t, LANES), F32)
    lr16 = jnp.concatenate([lr_ref[...], pad], axis=0).astype(BF16)
    x = jnp.dot(lr16, wa_ref[...], preferred_element_type=F32)[0:n_t] + ba_ref[...]
    g = _log2_gate(x)
    rowi = lax.broadcasted_iota(jnp.int32, (n_t, 1), 0)
    gcum = jnp.zeros_like(g)
    for l in range(n_t):
        gcum = gcum + jnp.where(rowi >= l, g[l:l + 1, :], 0.0)
    qs = q_ref[...].astype(F32) * (GLA_DK ** -0.5)
    kf = k_ref[...].astype(F32)
    vf = v_ref[...].astype(F32)
    s_old = s0_ref[...]

    o = jnp.zeros((n_t, GLA_DV), F32)
    for j in range(n_t):
        tt = qs * kf[j:j + 1, :] * jnp.exp2(gcum - gcum[j:j + 1, :])
        colv = jnp.where(rowi >= j, jnp.sum(tt, axis=-1, keepdims=True), 0.0)
        o = o + colv * vf[j:j + 1, :]

    zpad = jnp.zeros((n_t, GLA_DK), F32)
    q_hat = jnp.concatenate([qs * jnp.exp2(gcum), zpad], axis=0).astype(BF16)
    o = o + jnp.dot(q_hat, s_old.astype(BF16), preferred_element_type=F32)[0:n_t]

    g_last = gcum[n_t - 1:n_t, :]
    k_hat = jnp.concatenate([kf * jnp.exp2(g_last - gcum), zpad], axis=0).astype(BF16)
    v16 = jnp.concatenate([vf, jnp.zeros((n_t, GLA_DV), F32)], axis=0).astype(BF16)
    kv = lax.dot_general(k_hat, v16, (((0,), (0,)), ((), ())), preferred_element_type=F32)
    decay = jnp.exp2(jnp.broadcast_to(g_last, (LANES, GLA_DK)).T[:, 0:1])
    s_out_ref[...] = decay * s_old + kv

    o_n = o * _rms_scale(o) * gn_ref[...]
    r_h = r_ref[...].astype(F32)
    o_ref[...] = o_n * (r_h / (1.0 + jnp.exp(-r_h)))


def _gla_sample(p, lr, w_a2p, b_a, g_norm, s0, *, n_t, n_seq):
    nb = n_seq
    nh = GLA_HEADS
    o, s = pl.pallas_call(
        functools.partial(_gla_sample_kernel, n_t=n_t),
        grid=(nb, GLA_HEADS),
        in_specs=[
            pl.BlockSpec((n_t, GLA_DK), lambda b, h: (b, h)),
            pl.BlockSpec((n_t, GLA_DK), lambda b, h: (b, nh + h)),
            pl.BlockSpec((n_t, GLA_DV), lambda b, h: (b, nh + h)),
            pl.BlockSpec((n_t, GLA_DV), lambda b, h: (b, 2 * nh + h)),
            pl.BlockSpec((n_t, LANES), lambda b, h: (b, 0)),
            pl.BlockSpec((LANES, GLA_DK), lambda b, h: (0, h)),
            pl.BlockSpec((1, GLA_DK), lambda b, h: (0, h)),
            pl.BlockSpec((1, GLA_DV), lambda b, h: (0, 0)),
            pl.BlockSpec((None, None, GLA_DK, GLA_DV), lambda b, h: (b, h, 0, 0)),
        ],
        out_specs=[
            pl.BlockSpec((n_t, GLA_DV), lambda b, h: (b, h)),
            pl.BlockSpec((None, None, GLA_DK, GLA_DV), lambda b, h: (b, h, 0, 0)),
        ],
        out_shape=[
            jax.ShapeDtypeStruct((nb * n_t, GLA_HEADS * GLA_DV), F32),
            jax.ShapeDtypeStruct((nb,) + s0.shape[1:], F32),
        ],
        compiler_params=_cparams(2), name="gla_sample",
    )(p, p, p, p, lr, w_a2p, b_a.reshape(1, -1).astype(F32), g_norm.reshape(1, -1).astype(F32), s0)
    return o, s


def kernel(x_prompt, x_sample, cache_k, cache_v, page_table, state_gla, norm_mix, sb_w_qkv, sb_bias, sb_w_o,
           gla_w_in, gla_w_a2, gla_b_a, gla_g_norm, gla_w_o, norm_ffn, w_up, w_down, norm_final):
    bp, tp, d = x_prompt.shape
    bs, ts, _ = x_sample.shape
    assert bp == 1
    hd = SB_HEADS * SB_HEAD_DIM
    xp = x_prompt.reshape(bp * tp, d)
    xs = x_sample.reshape(bs * ts, d)
    scale = SB_HEAD_DIM ** -0.5 * LOG2E
    nk = GLA_HEADS * GLA_DK
    nv = GLA_HEADS * GLA_DV

    wqkv = sb_w_qkv.astype(BF16)
    wo = sb_w_o.astype(BF16)
    g0 = norm_mix[0]

    q_p, k_p, k_pb, v_p, v_pb = _proj(
        xp, wqkv, layer=0, gain=g0, name="sb_qkv_prompt",
        outs=[(BF16, 0, hd, scale), (F32, hd, hd, None), (BF16, hd, hd, None),
              (F32, 2 * hd, hd, None), (BF16, 2 * hd, hd, None)])
    o_p = _sb_attention_prompt(q_p, k_pb, v_pb, sb_bias[0])
    (xp,) = _proj(o_p, wo, layer=0, res=xp, tm=1024, name="sb_o_prompt")

    q_s, k_s, v_s = _proj(xs, wqkv, layer=0, gain=g0, name="sb_qkv_sample",
                          outs=[(F32, 0, hd, scale), (F32, hd, hd, None), (F32, 2 * hd, hd, None)])
    page = cache_k.shape[2]
    o_s = _sb_attention_sample(q_s, k_s, v_s, cache_k.reshape(-1, page * SB_HEADS, SB_HEAD_DIM),
                               cache_v.reshape(-1, page * SB_HEADS, SB_HEAD_DIM), page_table, sb_bias[0],
                               n_q=ts, page=page)
    (xs,) = _proj(o_s, wo, layer=0, res=xs, name="sb_o_sample")

    wu, wd = w_up.astype(BF16), w_down.astype(BF16)
    xp = _mlp(xp, norm_ffn[0], wu, wd, 0, norm_final, final_norm=False, name="mlp0_prompt")
    xs = _mlp(xs, norm_ffn[0], wu, wd, 0, norm_final, final_norm=False, name="mlp0_sample")

    w_in = gla_w_in[0]
    w_main = jnp.concatenate([w_in[:, :2 * nk + nv], w_in[:, 2 * nk + nv + GLA_GATE_RANK:]], axis=1).astype(BF16)
    w_lr = jnp.pad(w_in[:, 2 * nk + nv:2 * nk + nv + GLA_GATE_RANK],
                   ((0, 0), (0, LANES - GLA_GATE_RANK))).astype(BF16)
    w_a2p = jnp.pad(gla_w_a2[0], ((0, LANES - GLA_GATE_RANK), (0, 0))).astype(BF16)
    w_go = gla_w_o.astype(BF16)
    g1 = norm_mix[1]
    n_main = w_main.shape[1]

    (p_p,) = _proj(xp, w_main, gain=g1, outs=[(BF16, 0, n_main, None)], name="gla_in_prompt")
    (lr_p,) = _proj(xp, w_lr, gain=g1, name="gla_lr_prompt")
    go_p, s_p = _gla_prompt(p_p, lr_p, w_a2p, gla_b_a[0], gla_g_norm[0])
    (xp,) = _proj(go_p, w_go, layer=0, res=xp, tm=1024, name="gla_o_prompt")

    (p_s,) = _proj(xs, w_main, gain=g1, name="gla_in_sample")
    (lr_s,) = _proj(xs, w_lr, gain=g1, name="gla_lr_sample")
    s0 = state_gla.reshape(-1, GLA_HEADS, GLA_DK, GLA_DV)
    go_s, s_s = _gla_sample(p_s, lr_s, w_a2p, gla_b_a[0], gla_g_norm[0], s0, n_t=ts, n_seq=bs)
    (xs,) = _proj(go_s, w_go, layer=0, res=xs, name="gla_o_sample")

    y_p = _mlp(xp, norm_ffn[1], wu, wd, 1, norm_final, final_norm=True, name="mlp1_prompt")
    y_s = _mlp(xs, norm_ffn[1], wu, wd, 1, norm_final, final_norm=True, name="mlp1_sample")

    return (
        y_p.reshape(bp, tp, d),
        y_s.reshape(bs, ts, d),
        k_p.reshape(1, bp, tp, SB_HEADS, SB_HEAD_DIM),
        v_p.reshape(1, bp, tp, SB_HEADS, SB_HEAD_DIM),
        k_s.reshape(1, bs, ts, SB_HEADS, SB_HEAD_DIM),
        v_s.reshape(1, bs, ts, SB_HEADS, SB_HEAD_DIM),
        s_p.reshape(1, bp, GLA_HEADS, GLA_DK, GLA_DV),
        s_s.reshape(1, bs, GLA_HEADS, GLA_DK, GLA_DV),
    )
```

```python
import functools

import numpy as np
import jax
import jax.numpy as jnp
from jax import lax
from jax.experimental import pallas as pl
from jax.experimental.pallas import tpu as pltpu

F32 = jnp.float32
BF16 = jnp.bfloat16

EPS = 1e-6
SB_HEADS = 16
SB_HEAD_DIM = 128
GLA_HEADS = 4
GLA_DK = 256
GLA_DV = 512
GLA_GATE_RANK = 16
GLA_TAU = 16.0
GLA_CHUNK = 64
GLA_SUB = 16
SAMPLE_PAGES_PER_STEP = 8
LANES = 128
LOG2E = 1.4426950408889634
VMEM_LIMIT_BYTES = 56 * 1024 * 1024


def _cparams(n_grid_dims):
    return pltpu.CompilerParams(
        dimension_semantics=("arbitrary",) * n_grid_dims,
        vmem_limit_bytes=VMEM_LIMIT_BYTES)


def _log2_gate(x):
    softplus_neg = jnp.maximum(-x, 0.0) + jnp.log(1.0 + jnp.exp(-jnp.abs(x)))
    return softplus_neg * (-LOG2E / GLA_TAU)


def _rms_scale(x):
    return lax.rsqrt(jnp.mean(x * x, axis=-1, keepdims=True) + EPS)


def _proj_kernel(*refs, has_norm, has_res, sections, n_tiles):
    it = iter(refs)
    x_ref = next(it)
    g_ref = next(it) if has_norm else None
    w_ref = next(it)
    res_ref = next(it) if has_res else None
    out_refs = [next(it) for _ in sections]
    h_ref = next(it) if has_norm else None
    j = pl.program_id(1)

    if has_norm:
        @pl.when(j == 0)
        def _():
            x = x_ref[...]
            h_ref[...] = (x * _rms_scale(x) * g_ref[...]).astype(BF16)
        h = h_ref[...]
    else:
        h = x_ref[...].astype(BF16)
    y = jnp.dot(h, w_ref[...], preferred_element_type=F32)
    if has_res:
        y = y + res_ref[...]
    for (first, count, scale), o_ref in zip(sections, out_refs):
        def write(o_ref=o_ref, scale=scale):
            o_ref[...] = (y if scale is None else y * scale).astype(o_ref.dtype)
        if first == 0 and count == n_tiles:
            write()
        else:
            pl.when((j >= first) & (j < first + count))(write)


def _proj(x, w, *, gain=None, res=None, outs=None, layer=None, tm=512, tn=1024, name="proj"):
    m, kdim = x.shape
    n = w.shape[-1]
    tm = min(tm, m)
    tn = min(tn, n)
    assert m % tm == 0 and n % tn == 0
    n_tiles = n // tn
    outs = outs or [(F32, 0, n, None)]
    has_norm = gain is not None
    has_res = res is not None
    assert not has_res or (len(outs) == 1 and outs[0][1] == 0 and outs[0][2] == n)
    in_specs = [pl.BlockSpec((tm, kdim), lambda i, j: (i, 0))]
    args = [x]
    if has_norm:
        in_specs.append(pl.BlockSpec((1, kdim), lambda i, j: (0, 0)))
        args.append(gain.reshape(1, kdim).astype(F32))
    if layer is None:
        in_specs.append(pl.BlockSpec((kdim, tn), lambda i, j: (0, j)))
    else:
        in_specs.append(pl.BlockSpec((None, kdim, tn), lambda i, j: (layer, 0, j)))
    args.append(w)
    if has_res:
        in_specs.append(pl.BlockSpec((tm, tn), lambda i, j: (i, j)))
        args.append(res)
    sections, out_shape, out_specs = [], [], []
    for dt, col0, cols, scale in outs:
        assert col0 % tn == 0 and cols % tn == 0
        first, count = col0 // tn, cols // tn
        sections.append((first, count, scale))
        out_shape.append(jax.ShapeDtypeStruct((m, cols), dt))
        out_specs.append(pl.BlockSpec(
            (tm, tn), lambda i, j, first=first, count=count: (i, jnp.clip(j - first, 0, count - 1))))
    scratch = [pltpu.VMEM((tm, kdim), BF16)] if has_norm else []
    return pl.pallas_call(
        functools.partial(_proj_kernel, has_norm=has_norm, has_res=has_res,
                          sections=tuple(sections), n_tiles=n_tiles),
        grid=(m // tm, n_tiles),
        in_specs=in_specs, out_specs=out_specs, out_shape=out_shape,
        scratch_shapes=scratch, compiler_params=_cparams(2), name=name,
    )(*args)


def _mlp_kernel(x_ref, g_ref, wu_ref, wd_ref, gf_ref, o_ref, h_ref, *, final_norm):
    c = pl.program_id(1)

    @pl.when(c == 0)
    def _():
        x = x_ref[...]
        h_ref[...] = (x * _rms_scale(x) * g_ref[...]).astype(BF16)
        o_ref[...] = x

    u = jnp.maximum(jnp.dot(h_ref[...], wu_ref[...], preferred_element_type=F32), 0.0)
    o_ref[...] += jnp.dot((u * u).astype(BF16), wd_ref[...], preferred_element_type=F32)

    if final_norm:
        @pl.when(c == pl.num_programs(1) - 1)
        def _():
            y = o_ref[...]
            o_ref[...] = y * _rms_scale(y) * gf_ref[...]


def _mlp(x, gain, w_up, w_down, layer, gain_final, *, final_norm, tm=512, tf=1024, name="mlp"):
    m, d = x.shape
    dff = w_up.shape[2]
    tm = min(tm, m)
    assert m % tm == 0 and dff % tf == 0
    return pl.pallas_call(
        functools.partial(_mlp_kernel, final_norm=final_norm),
        grid=(m // tm, dff // tf),
        in_specs=[
            pl.BlockSpec((tm, d), lambda i, c: (i, 0)),
            pl.BlockSpec((1, d), lambda i, c: (0, 0)),
            pl.BlockSpec((None, d, tf), lambda i, c: (layer, 0, c)),
            pl.BlockSpec((None, tf, d), lambda i, c: (layer, c, 0)),
            pl.BlockSpec((1, d), lambda i, c: (0, 0)),
        ],
        out_specs=pl.BlockSpec((tm, d), lambda i, c: (i, 0)),
        out_shape=jax.ShapeDtypeStruct((m, d), F32),
        scratch_shapes=[pltpu.VMEM((tm, d), BF16)],
        compiler_params=_cparams(2), name=name,
    )(x, gain.reshape(1, d).astype(F32), w_up, w_down, gain_final.reshape(1, d).astype(F32))


def _softplus2(z2):
    neg_abs = lax.bitcast_convert_type(lax.bitcast_convert_type(z2, jnp.uint32) | jnp.uint32(0x80000000), F32)
    return jnp.maximum(z2, 0.0) + jnp.log2(1.0 + jnp.exp2(neg_abs))


def _sb_block(q, kb, vb, ntri, bias2, carry, allowed_top):
    def mask_top(x):
        if allowed_top is None:
            return x
        nt = allowed_top.shape[0]
        top = jnp.where(allowed_top, x[:nt], 0.0)
        return top if x.shape[0] == nt else jnp.concatenate([top, x[nt:]], axis=0)

    z = lax.dot_general(q, kb, (((1,), (1,)), ((), ())), preferred_element_type=F32) + bias2
    sp = mask_top(_softplus2(z))
    cum = jnp.dot(sp.astype(BF16), ntri, preferred_element_type=F32)
    w = mask_top(jnp.exp2((z + carry) + cum))
    pv = jnp.dot(w.astype(BF16), vb, preferred_element_type=F32)
    return pv, carry + cum[:, 0:1]


def _sb_prompt_kernel(bias_ref, q_ref, k_ref, v_ref, ntri_ref, o_ref, acc_ref, carry_ref, *, bq, bk, unroll):
    h = pl.program_id(0)
    qi = pl.program_id(1)
    bias2 = bias_ref[h] * LOG2E
    ntri = ntri_ref[...]
    n_sub = bq // bk
    row = lax.broadcasted_iota(jnp.int32, (bk, bk), 0)
    col = lax.broadcasted_iota(jnp.int32, (bk, bk), 1)
    allowed_top = col < row

    for c in reversed(range(n_sub)):
        r0 = c * bk
        k0 = pl.multiple_of(qi * bq + r0, bk)
        if c == n_sub - 1:
            carry_in = jnp.zeros((bk, 1), F32)
        else:
            carry_in = jnp.concatenate([jnp.zeros((bk, 1), F32), carry_ref[r0 + bk:, :]], axis=0)
        pv, carry = _sb_block(q_ref[r0:, :], k_ref[pl.ds(k0, bk), :], v_ref[pl.ds(k0, bk), :], ntri, bias2,
                              carry_in, allowed_top)
        acc_ref[r0:r0 + bk, :] = pv[:bk]
        if c < n_sub - 1:
            acc_ref[r0 + bk:, :] += pv[bk:]
        carry_ref[r0:, :] = carry

    def body(t, _):
        carry = carry_ref[...]
        pv_sum = None
        for u in range(unroll):
            k0 = pl.multiple_of(qi * bq - (unroll * t + u + 1) * bk, bk)
            pv, carry = _sb_block(q_ref[...], k_ref[pl.ds(k0, bk), :], v_ref[pl.ds(k0, bk), :], ntri, bias2,
                                  carry, None)
            pv_sum = pv if pv_sum is None else pv_sum + pv
        acc_ref[...] += pv_sum
        carry_ref[...] = carry
        return 0

    assert n_sub % unroll == 0
    lax.fori_loop(0, qi * (n_sub // unroll), body, 0)
    o_ref[...] = acc_ref[...].astype(o_ref.dtype)


def _tri_ge(n, value=1.0):
    i = np.arange(n)
    return jnp.asarray((i[:, None] >= i[None, :]).astype(np.float32) * value, dtype=BF16)


def _sb_attention_prompt(q, k, v, bias, *, bq=1024, bk=256, unroll=4):
    t = q.shape[0]
    dh = SB_HEAD_DIM
    return pl.pallas_call(
        functools.partial(_sb_prompt_kernel, bq=bq, bk=bk, unroll=unroll),
        grid=(SB_HEADS, t // bq),
        in_specs=[
            pl.BlockSpec(memory_space=pltpu.SMEM),
            pl.BlockSpec((bq, dh), lambda h, i: (i, h)),
            pl.BlockSpec((t, dh), lambda h, i: (0, h)),
            pl.BlockSpec((t, dh), lambda h, i: (0, h)),
            pl.BlockSpec((bk, bk), lambda h, i: (0, 0)),
        ],
        out_specs=pl.BlockSpec((bq, dh), lambda h, i: (i, h)),
        out_shape=jax.ShapeDtypeStruct((t, SB_HEADS * dh), BF16),
        scratch_shapes=[pltpu.VMEM((bq, dh), F32), pltpu.VMEM((bq, 1), F32)],
        compiler_params=_cparams(2), name="sb_attn_prompt",
    )(bias.astype(F32), q, k, v, _tri_ge(bk, -1.0))


def _sb_sample_kernel(pt_ref, q_ref, k2_ref, v2_ref, *rest, n_q, page, pps):
    k_refs, v_refs = rest[:pps], rest[pps:2 * pps]
    ntri_ref, npar_ref, bias_ref, o_ref, acc_ref, carry_ref, kpk_ref, vpk_ref = rest[2 * pps:]
    s = pl.program_id(1)
    dh = SB_HEAD_DIM
    n_pairs = SB_HEADS // 2
    prow = 2 * n_q
    nrow = SB_HEADS * n_q
    width = 2 * page

    q_rows = jnp.concatenate([q_ref[:, h * dh:(h + 1) * dh] for h in range(SB_HEADS)], axis=0).astype(BF16)
    row = lax.broadcasted_iota(jnp.int32, (nrow, width), 0)
    col = lax.broadcasted_iota(jnp.int32, (nrow, width), 1)
    head_parity = (row // n_q) % 2

    def step(k_of_pair, v_of_pair, allowed, cum_ref, total_cols):
        z = jnp.concatenate(
            [lax.dot_general(q_rows[a * prow:(a + 1) * prow], k_of_pair(a), (((1,), (1,)), ((), ())),
                             preferred_element_type=F32) for a in range(n_pairs)], axis=0) + bias_ref[...]
        sp = jnp.where(allowed, _softplus2(z), 0.0)
        cum = jnp.dot(sp.astype(BF16), cum_ref[...], preferred_element_type=F32)
        w = jnp.where(allowed, jnp.exp2(z + cum + carry_ref[...]), 0.0)
        total = cum[:, 0:1]
        for c in range(1, total_cols):
            total = total + cum[:, c:c + 1]
        carry_ref[...] += total
        wb = w.astype(BF16)
        for a in range(n_pairs):
            acc_ref[a * prow:(a + 1) * prow, :] += jnp.dot(wb[a * prow:(a + 1) * prow], v_of_pair(a),
                                                           preferred_element_type=F32)

    @pl.when(s == 0)
    def _():
        acc_ref[...] = jnp.zeros_like(acc_ref)
        carry_ref[...] = jnp.zeros_like(carry_ref)
        pad = jnp.zeros((width - prow, dh), F32)

        def new_pair(ref):
            return lambda a: jnp.concatenate([ref[:, 2 * a * dh:(2 * a + 1) * dh],
                                              ref[:, (2 * a + 1) * dh:(2 * a + 2) * dh], pad], axis=0).astype(BF16)

        allowed = (col // n_q == head_parity) & (col % n_q < row % n_q)
        step(new_pair(k2_ref), new_pair(v2_ref), allowed, ntri_ref, 1)

    @pl.when(s > 0)
    def _():
        allowed = col % 2 == head_parity
        for p in range(pps):
            kpk_ref[p] = pltpu.bitcast(k_refs[p][...].astype(BF16), jnp.uint32)
            vpk_ref[p] = pltpu.bitcast(v_refs[p][...].astype(BF16), jnp.uint32)

            def cache_pair(ref, p=p):
                return lambda a: pltpu.bitcast(ref[p, pl.ds(a, page, stride=n_pairs), :], BF16)

            step(cache_pair(kpk_ref), cache_pair(vpk_ref), allowed, npar_ref, 2)

    @pl.when(s == pl.num_programs(1) - 1)
    def _():
        for h in range(SB_HEADS):
            o_ref[:, h * dh:(h + 1) * dh] = acc_ref[h * n_q:(h + 1) * n_q, :]


def _sb_attention_sample(q, k2, v2, cache_k, cache_v, page_table, bias, *, n_q, page):
    nb, n_pages = page_table.shape
    pps = SAMPLE_PAGES_PER_STEP
    assert n_pages % pps == 0 and SB_HEADS % 2 == 0 and 2 * n_q <= 2 * page
    hd = SB_HEADS * SB_HEAD_DIM
    nrow = SB_HEADS * n_q
    prow = page * SB_HEADS
    width = 2 * page

    def cache_map(p):
        def index_map(b, s, pt):
            return (pt[b, n_pages - pps * (jnp.maximum(s, 1) - 1) - 1 - p], 0, 0)
        return index_map

    i = np.arange(width)
    ge = i[:, None] >= i[None, :]
    same_parity = (i[:, None] - i[None, :]) % 2 == 0
    ntri = jnp.asarray(-(ge).astype(np.float32), dtype=BF16)
    npar = jnp.asarray(-(ge & same_parity).astype(np.float32), dtype=BF16)
    bias_rows = jnp.broadcast_to(jnp.repeat(bias.astype(F32) * LOG2E, n_q)[:, None], (nrow, width))
    small = pl.BlockSpec((n_q, hd), lambda b, s, pt: (b, 0))
    const = lambda shape: pl.BlockSpec(shape, lambda b, s, pt: (0, 0))
    grid_spec = pltpu.PrefetchScalarGridSpec(
        num_scalar_prefetch=1,
        grid=(nb, n_pages // pps + 1),
        in_specs=([small, small, small]
                  + [pl.BlockSpec((None, prow, SB_HEAD_DIM), cache_map(p)) for p in range(pps)] * 2
                  + [const((width, width)), const((width, width)), const((nrow, width))]),
        out_specs=small,
        scratch_shapes=[pltpu.VMEM((nrow, SB_HEAD_DIM), F32), pltpu.VMEM((nrow, 1), F32),
                        pltpu.VMEM((pps, prow // 2, SB_HEAD_DIM), jnp.uint32),
                        pltpu.VMEM((pps, prow // 2, SB_HEAD_DIM), jnp.uint32)],
    )
    return pl.pallas_call(
        functools.partial(_sb_sample_kernel, n_q=n_q, page=page, pps=pps),
        grid_spec=grid_spec,
        out_shape=jax.ShapeDtypeStruct((nb * n_q, hd), F32),
        compiler_params=_cparams(2), name="sb_attn_sample",
    )(page_table, q, k2, v2, *([cache_k] * pps), *([cache_v] * pps), ntri, npar, bias_rows)


def _gla_masks(r, c, sc):
    i = np.arange(r)
    chunk = i // c
    cstart = chunk * c
    sub0 = (i // sc) * sc
    same = chunk[:, None] == chunk[None, :]
    il, ll = i[:, None], i[None, :]
    nb = c // sc
    mats = [
        same & (ll <= il),
        (sub0[:, None] < ll) & (ll <= il),
        same & (il < ll),
    ]
    for blk in range(1, nb):
        mats.append(same & (il < ll) & (ll <= (cstart + sc * blk)[:, None]))
    sums = np.concatenate(mats, axis=0).astype(np.float32)
    sel = []
    for blk in range(1, nb):
        sel.append(same & (((i % c) // sc) == blk)[:, None] & (ll < (cstart + sc * blk)[:, None]))
    sel = np.stack(sel, axis=0).astype(np.float32) if sel else np.zeros((1, r, r), np.float32)
    return jnp.asarray(sums, dtype=BF16), jnp.asarray(sel, dtype=F32)


def _gla_prompt_kernel(qk_ref, v_ref, r_ref, lr_ref, wa_ref, ba_ref, gn_ref, sums_ref, sel_ref,
                       o_ref, s_out_ref, s_ref, g_scr, k_scr, a_scr, *, rows, chunk, sub):
    t = pl.program_id(0)
    nk = GLA_HEADS * GLA_DK
    n_chunks = rows // chunk
    n_blk = chunk // sub
    n_sub = rows // sub

    @pl.when(t == 0)
    def _():
        s_ref[...] = jnp.zeros_like(s_ref)

    lr = lr_ref[...].astype(BF16)
    for h in range(GLA_HEADS):
        ksl = slice(h * GLA_DK, (h + 1) * GLA_DK)
        vsl = slice(h * GLA_DV, (h + 1) * GLA_DV)
        x = jnp.dot(lr, wa_ref[:, ksl], preferred_element_type=F32) + ba_ref[:, ksl]
        g = _log2_gate(x)
        g_hi = g.astype(BF16)
        g_lo = (g - g_hi.astype(F32)).astype(BF16)
        e = (jnp.dot(sums_ref[...], g_hi, preferred_element_type=F32)
             + jnp.dot(sums_ref[...], g_lo, preferred_element_type=F32))
        gcum = e[0:rows]
        qs = qk_ref[:, ksl].astype(F32) * (GLA_DK ** -0.5)
        kf = qk_ref[:, nk + h * GLA_DK: nk + (h + 1) * GLA_DK].astype(F32)
        vb = v_ref[:, vsl]
        q_hat = (qs * jnp.exp2(gcum)).astype(BF16)
        q_til = (qs * jnp.exp2(e[rows:2 * rows])).astype(BF16)
        k_hat = (kf * jnp.exp2(e[2 * rows:3 * rows])).astype(BF16)

        a = jnp.zeros((rows, rows), F32)
        for b in range(1, n_blk):
            k_til = (kf * jnp.exp2(e[(2 + b) * rows:(3 + b) * rows])).astype(BF16)
            p = lax.dot_general(q_til, k_til, (((1,), (1,)), ((), ())), preferred_element_type=F32)
            a = a + p * sel_ref[b - 1]
        a_scr[...] = a

        g_scr[...] = gcum
        k_scr[...] = kf
        lane = lax.broadcasted_iota(jnp.int32, (sub, LANES), 1)
        rowi = lax.broadcasted_iota(jnp.int32, (sub, 1), 0)
        for sci in range(n_sub):
            r0 = sci * sub
            q_s = qs[r0:r0 + sub]
            g_s = gcum[r0:r0 + sub]
            lane0 = r0 % LANES
            tile = r0 // LANES
            blk_a = jnp.zeros((sub, LANES), F32)
            for j in range(sub):
                kj = k_scr[r0 + j:r0 + j + 1, :]
                gj = g_scr[r0 + j:r0 + j + 1, :]
                tt = q_s * kj * jnp.exp2(g_s - gj)
                colv = jnp.sum(tt, axis=-1, keepdims=True)
                colv = jnp.where(rowi >= j, colv, 0.0)
                blk_a = jnp.where(lane == lane0 + j, colv, blk_a)
            a_scr[r0:r0 + sub, tile * LANES:(tile + 1) * LANES] += blk_a

        o_intra = jnp.dot(a_scr[...].astype(BF16), vb, preferred_element_type=F32)

        gcum_t = gcum.T
        outs = []
        for c in range(n_chunks):
            c0 = c * chunk
            s_old = s_ref[h]
            o_c = o_intra[c0:c0 + chunk] + jnp.dot(q_hat[c0:c0 + chunk], s_old.astype(BF16),
                                                  preferred_element_type=F32)
            decay = jnp.exp2(gcum_t[:, c0 + chunk - 1:c0 + chunk])
            kv = lax.dot_general(k_hat[c0:c0 + chunk], vb[c0:c0 + chunk], (((0,), (0,)), ((), ())),
                                 preferred_element_type=F32)
            s_ref[h] = decay * s_old + kv
            outs.append(o_c)
        o_h = jnp.concatenate(outs, axis=0) if n_chunks > 1 else outs[0]
        o_n = o_h * _rms_scale(o_h) * gn_ref[...]
        r_h = r_ref[:, vsl].astype(F32)
        o_ref[:, vsl] = (o_n * (r_h / (1.0 + jnp.exp(-r_h)))).astype(o_ref.dtype)

    @pl.when(t == pl.num_programs(0) - 1)
    def _():
        s_out_ref[...] = s_ref[...]


def _gla_prompt(p, lr, w_a2p, b_a, g_norm, *, rows=256):
    t = p.shape[0]
    nv = GLA_HEADS * GLA_DV
    sums, sel = _gla_masks(rows, GLA_CHUNK, GLA_SUB)
    n_sums = sums.shape[0]
    o, s = pl.pallas_call(
        functools.partial(_gla_prompt_kernel, rows=rows, chunk=GLA_CHUNK, sub=GLA_SUB),
        grid=(t // rows,),
        in_specs=[
            pl.BlockSpec((rows, nv), lambda i: (i, 0)),
            pl.BlockSpec((rows, nv), lambda i: (i, 1)),
            pl.BlockSpec((rows, nv), lambda i: (i, 2)),
            pl.BlockSpec((rows, LANES), lambda i: (i, 0)),
            pl.BlockSpec((LANES, GLA_HEADS * GLA_DK), lambda i: (0, 0)),
            pl.BlockSpec((1, GLA_HEADS * GLA_DK), lambda i: (0, 0)),
            pl.BlockSpec((1, GLA_DV), lambda i: (0, 0)),
            pl.BlockSpec((n_sums, rows), lambda i: (0, 0)),
            pl.BlockSpec(sel.shape, lambda i: (0, 0, 0)),
        ],
        out_specs=[
            pl.BlockSpec((rows, nv), lambda i: (i, 0)),
            pl.BlockSpec((GLA_HEADS, GLA_DK, GLA_DV), lambda i: (0, 0, 0)),
        ],
        out_shape=[
            jax.ShapeDtypeStruct((t, nv), BF16),
            jax.ShapeDtypeStruct((GLA_HEADS, GLA_DK, GLA_DV), F32),
        ],
        scratch_shapes=[
            pltpu.VMEM((GLA_HEADS, GLA_DK, GLA_DV), F32),
            pltpu.VMEM((rows, GLA_DK), F32),
            pltpu.VMEM((rows, GLA_DK), F32),
            pltpu.VMEM((rows, rows), F32),
        ],
        compiler_params=_cparams(1), name="gla_prompt",
    )(p, p, p, lr, w_a2p, b_a.reshape(1, -1).astype(F32), g_norm.reshape(1, -1).astype(F32), sums, sel)
    return o, s


def _gla_sample_kernel(q_ref, k_ref, v_ref, r_ref, lr_ref, wa_ref, ba_ref, gn_ref, s0_ref, o_ref, s_out_ref,
                       *, n_t):
    pad = jnp.zeros((n_t, LANES), F32)
    lr16 = jnp.concatenate([lr_ref[...], pad], axis=0).astype(BF16)
    x = jnp.dot(lr16, wa_ref[...], preferred_element_type=F32)[0:n_t] + ba_ref[...]
    g = _log2_gate(x)
    rowi = lax.broadcasted_iota(jnp.int32, (n_t, 1), 0)
    gcum = jnp.zeros_like(g)
    for l in range(n_t):
        gcum = gcum + jnp.where(rowi >= l, g[l:l + 1, :], 0.0)
    qs = q_ref[...].astype(F32) * (GLA_DK ** -0.5)
    kf = k_ref[...].astype(F32)
    vf = v_ref[...].astype(F32)
    s_old = s0_ref[...]

    o = jnp.zeros((n_t, GLA_DV), F32)
    for j in range(n_t):
        tt = qs * kf[j:j + 1, :] * jnp.exp2(gcum - gcum[j:j + 1, :])
        colv = jnp.where(rowi >= j, jnp.sum(tt, axis=-1, keepdims=True), 0.0)
        o = o + colv * vf[j:j + 1, :]

    zpad = jnp.zeros((n_t, GLA_DK), F32)
    q_hat = jnp.concatenate([qs * jnp.exp2(gcum), zpad], axis=0).astype(BF16)
    o = o + jnp.dot(q_hat, s_old.astype(BF16), preferred_element_type=F32)[0:n_t]

    g_last = gcum[n_t - 1:n_t, :]
    k_hat = jnp.concatenate([kf * jnp.exp2(g_last - gcum), zpad], axis=0).astype(BF16)
    v16 = jnp.concatenate([vf, jnp.zeros((n_t, GLA_DV), F32)], axis=0).astype(BF16)
    kv = lax.dot_general(k_hat, v16, (((0,), (0,)), ((), ())), preferred_element_type=F32)
    decay = jnp.exp2(jnp.broadcast_to(g_last, (LANES, GLA_DK)).T[:, 0:1])
    s_out_ref[...] = decay * s_old + kv

    o_n = o * _rms_scale(o) * gn_ref[...]
    r_h = r_ref[...].astype(F32)
    o_ref[...] = o_n * (r_h / (1.0 + jnp.exp(-r_h)))


def _gla_sample(p, lr, w_a2p, b_a, g_norm, s0, *, n_t, n_seq):
    nb = n_seq
    nh = GLA_HEADS
    o, s = pl.pallas_call(
        functools.partial(_gla_sample_kernel, n_t=n_t),
        grid=(nb, GLA_HEADS),
        in_specs=[
            pl.BlockSpec((n_t, GLA_DK), lambda b, h: (b, h)),
            pl.BlockSpec((n_t, GLA_DK), lambda b, h: (b, nh + h)),
            pl.BlockSpec((n_t, GLA_DV), lambda b, h: (b, nh + h)),
            pl.BlockSpec((n_t, GLA_DV), lambda b, h: (b, 2 * nh + h)),
            pl.BlockSpec((n_t, LANES), lambda b, h: (b, 0)),
            pl.BlockSpec((LANES, GLA_DK), lambda b, h: (0, h)),
            pl.BlockSpec((1, GLA_DK), lambda b, h: (0, h)),
            pl.BlockSpec((1, GLA_DV), lambda b, h: (0, 0)),
            pl.BlockSpec((None, None, GLA_DK, GLA_DV), lambda b, h: (b, h, 0, 0)),
        ],
        out_specs=[
            pl.BlockSpec((n_t, GLA_DV), lambda b, h: (b, h)),
            pl.BlockSpec((None, None, GLA_DK, GLA_DV), lambda b, h: (b, h, 0, 0)),
        ],
        out_shape=[
            jax.ShapeDtypeStruct((nb * n_t, GLA_HEADS * GLA_DV), F32),
            jax.ShapeDtypeStruct((nb,) + s0.shape[1:], F32),
        ],
        compiler_params=_cparams(2), name="gla_sample",
    )(p, p, p, p, lr, w_a2p, b_a.reshape(1, -1).astype(F32), g_norm.reshape(1, -1).astype(F32), s0)
    return o, s


def kernel(x_prompt, x_sample, cache_k, cache_v, page_table, state_gla, norm_mix, sb_w_qkv, sb_bias, sb_w_o,
           gla_w_in, gla_w_a2, gla_b_a, gla_g_norm, gla_w_o, norm_ffn, w_up, w_down, norm_final):
    bp, tp, d = x_prompt.shape
    bs, ts, _ = x_sample.shape
    assert bp == 1
    hd = SB_HEADS * SB_HEAD_DIM
    xp = x_prompt.reshape(bp * tp, d)
    xs = x_sample.reshape(bs * ts, d)
    scale = SB_HEAD_DIM ** -0.5 * LOG2E
    nk = GLA_HEADS * GLA_DK
    nv = GLA_HEADS * GLA_DV

    wqkv = sb_w_qkv.astype(BF16)
    wo = sb_w_o.astype(BF16)
    g0 = norm_mix[0]

    q_p, k_p, k_pb, v_p, v_pb = _proj(
        xp, wqkv, layer=0, gain=g0, name="sb_qkv_prompt",
        outs=[(BF16, 0, hd, scale), (F32, hd, hd, None), (BF16, hd, hd, None),
              (F32, 2 * hd, hd, None), (BF16, 2 * hd, hd, None)])
    o_p = _sb_attention_prompt(q_p, k_pb, v_pb, sb_bias[0])
    (xp,) = _proj(o_p, wo, layer=0, res=xp, tm=1024, name="sb_o_prompt")

    q_s, k_s, v_s = _proj(xs, wqkv, layer=0, gain=g0, name="sb_qkv_sample",
                          outs=[(F32, 0, hd, scale), (F32, hd, hd, None), (F32, 2 * hd, hd, None)])
    page = cache_k.shape[2]
    o_s = _sb_attention_sample(q_s, k_s, v_s, cache_k.reshape(-1, page * SB_HEADS, SB_HEAD_DIM),
                               cache_v.reshape(-1, page * SB_HEADS, SB_HEAD_DIM), page_table, sb_bias[0],
                               n_q=ts, page=page)
    (xs,) = _proj(o_s, wo, layer=0, res=xs, name="sb_o_sample")

    wu, wd = w_up.astype(BF16), w_down.astype(BF16)
    xp = _mlp(xp, norm_ffn[0], wu, wd, 0, norm_final, final_norm=False, name="mlp0_prompt")
    xs = _mlp(xs, norm_ffn[0], wu, wd, 0, norm_final, final_norm=False, name="mlp0_sample")

    w_in = gla_w_in[0]
    w_main = jnp.concatenate([w_in[:, :2 * nk + nv].astype(BF16),
                              w_in[:, 2 * nk + nv + GLA_GATE_RANK:].astype(BF16)], axis=1)
    w_lr = jnp.pad(w_in[:, 2 * nk + nv:2 * nk + nv + GLA_GATE_RANK],
                   ((0, 0), (0, LANES - GLA_GATE_RANK))).astype(BF16)
    w_a2p = jnp.pad(gla_w_a2[0], ((0, LANES - GLA_GATE_RANK), (0, 0))).astype(BF16)
    w_go = gla_w_o.astype(BF16)
    g1 = norm_mix[1]
    n_main = w_main.shape[1]

    (p_p,) = _proj(xp, w_main, gain=g1, outs=[(BF16, 0, n_main, None)], tn=2048, name="gla_in_prompt")
    (lr_p,) = _proj(xp, w_lr, gain=g1, name="gla_lr_prompt")
    go_p, s_p = _gla_prompt(p_p, lr_p, w_a2p, gla_b_a[0], gla_g_norm[0])
    (xp,) = _proj(go_p, w_go, layer=0, res=xp, tm=1024, name="gla_o_prompt")

    (p_s,) = _proj(xs, w_main, gain=g1, name="gla_in_sample")
    (lr_s,) = _proj(xs, w_lr, gain=g1, name="gla_lr_sample")
    s0 = state_gla.reshape(-1, GLA_HEADS, GLA_DK, GLA_DV)
    go_s, s_s = _gla_sample(p_s, lr_s, w_a2p, gla_b_a[0], gla_g_norm[0], s0, n_t=ts, n_seq=bs)
    (xs,) = _proj(go_s, w_go, layer=0, res=xs, name="gla_o_sample")

    y_p = _mlp(xp, norm_ffn[1], wu, wd, 1, norm_final, final_norm=True, name="mlp1_prompt")
    y_s = _mlp(xs, norm_ffn[1], wu, wd, 1, norm_final, final_norm=True, name="mlp1_sample")

    return (
        y_p.reshape(bp, tp, d),
        y_s.reshape(bs, ts, d),
        k_p.reshape(1, bp, tp, SB_HEADS, SB_HEAD_DIM),
        v_p.reshape(1, bp, tp, SB_HEADS, SB_HEAD_DIM),
        k_s.reshape(1, bs, ts, SB_HEADS, SB_HEAD_DIM),
        v_s.reshape(1, bs, ts, SB_HEADS, SB_HEAD_DIM),
        s_p.reshape(1, bp, GLA_HEADS, GLA_DK, GLA_DV),
        s_s.reshape(1, bs, GLA_HEADS, GLA_DK, GLA_DV),
    )
```

```python
import functools

import numpy as np
import jax
import jax.numpy as jnp
from jax import lax
from jax.experimental import pallas as pl
from jax.experimental.pallas import tpu as pltpu

F32 = jnp.float32
BF16 = jnp.bfloat16

EPS = 1e-6
SB_HEADS = 16
SB_HEAD_DIM = 128
GLA_HEADS = 4
GLA_DK = 256
GLA_DV = 512
GLA_GATE_RANK = 16
GLA_TAU = 16.0
GLA_CHUNK = 64
GLA_SUB = 16
SAMPLE_PAGES_PER_STEP = 8
LANES = 128
LOG2E = 1.4426950408889634
VMEM_LIMIT_BYTES = 56 * 1024 * 1024


def _cparams(n_grid_dims):
    return pltpu.CompilerParams(
        dimension_semantics=("arbitrary",) * n_grid_dims,
        vmem_limit_bytes=VMEM_LIMIT_BYTES)


def _log2_gate(x):
    softplus_neg = jnp.maximum(-x, 0.0) + jnp.log(1.0 + jnp.exp(-jnp.abs(x)))
    return softplus_neg * (-LOG2E / GLA_TAU)


def _rms_scale(x):
    return lax.rsqrt(jnp.mean(x * x, axis=-1, keepdims=True) + EPS)


def _proj_kernel(*refs, has_norm, has_res, sections, n_tiles, w_t):
    it = iter(refs)
    x_ref = next(it)
    g_ref = next(it) if has_norm else None
    w_ref = next(it)
    res_ref = next(it) if has_res else None
    out_refs = [next(it) for _ in sections]
    h_ref = next(it) if has_norm else None
    j = pl.program_id(1)

    if has_norm:
        @pl.when(j == 0)
        def _():
            x = x_ref[...]
            h_ref[...] = (x * _rms_scale(x) * g_ref[...]).astype(BF16)
        h = h_ref[...]
    else:
        h = x_ref[...].astype(BF16)
    if w_t:
        y = lax.dot_general(h, w_ref[...], (((1,), (1,)), ((), ())), preferred_element_type=F32)
    else:
        y = jnp.dot(h, w_ref[...], preferred_element_type=F32)
    if has_res:
        y = y + res_ref[...]
    for (first, count, scale), o_ref in zip(sections, out_refs):
        def write(o_ref=o_ref, scale=scale):
            o_ref[...] = (y if scale is None else y * scale).astype(o_ref.dtype)
        if first == 0 and count == n_tiles:
            write()
        else:
            pl.when((j >= first) & (j < first + count))(write)


def _proj(x, w, *, gain=None, res=None, outs=None, layer=None, w_t=False, tm=512, tn=1024, name="proj"):
    m, kdim = x.shape
    n = w.shape[-2] if w_t else w.shape[-1]
    assert not (w_t and layer is not None)
    tm = min(tm, m)
    tn = min(tn, n)
    assert m % tm == 0 and n % tn == 0
    n_tiles = n // tn
    outs = outs or [(F32, 0, n, None)]
    has_norm = gain is not None
    has_res = res is not None
    assert not has_res or (len(outs) == 1 and outs[0][1] == 0 and outs[0][2] == n)
    in_specs = [pl.BlockSpec((tm, kdim), lambda i, j: (i, 0))]
    args = [x]
    if has_norm:
        in_specs.append(pl.BlockSpec((1, kdim), lambda i, j: (0, 0)))
        args.append(gain.reshape(1, kdim).astype(F32))
    if w_t:
        in_specs.append(pl.BlockSpec((tn, kdim), lambda i, j: (j, 0)))
    elif layer is None:
        in_specs.append(pl.BlockSpec((kdim, tn), lambda i, j: (0, j)))
    else:
        in_specs.append(pl.BlockSpec((None, kdim, tn), lambda i, j: (layer, 0, j)))
    args.append(w)
    if has_res:
        in_specs.append(pl.BlockSpec((tm, tn), lambda i, j: (i, j)))
        args.append(res)
    sections, out_shape, out_specs = [], [], []
    for dt, col0, cols, scale in outs:
        assert col0 % tn == 0 and cols % tn == 0
        first, count = col0 // tn, cols // tn
        sections.append((first, count, scale))
        out_shape.append(jax.ShapeDtypeStruct((m, cols), dt))
        out_specs.append(pl.BlockSpec(
            (tm, tn), lambda i, j, first=first, count=count: (i, jnp.clip(j - first, 0, count - 1))))
    scratch = [pltpu.VMEM((tm, kdim), BF16)] if has_norm else []
    return pl.pallas_call(
        functools.partial(_proj_kernel, has_norm=has_norm, has_res=has_res,
                          sections=tuple(sections), n_tiles=n_tiles, w_t=w_t),
        grid=(m // tm, n_tiles),
        in_specs=in_specs, out_specs=out_specs, out_shape=out_shape,
        scratch_shapes=scratch, compiler_params=_cparams(2), name=name,
    )(*args)


def _mlp_kernel(x_ref, g_ref, wu_ref, wd_ref, gf_ref, o_ref, h_ref, *, final_norm):
    c = pl.program_id(1)

    @pl.when(c == 0)
    def _():
        x = x_ref[...]
        h_ref[...] = (x * _rms_scale(x) * g_ref[...]).astype(BF16)
        o_ref[...] = x

    u = jnp.maximum(jnp.dot(h_ref[...], wu_ref[...], preferred_element_type=F32), 0.0)
    o_ref[...] += jnp.dot((u * u).astype(BF16), wd_ref[...], preferred_element_type=F32)

    if final_norm:
        @pl.when(c == pl.num_programs(1) - 1)
        def _():
            y = o_ref[...]
            o_ref[...] = y * _rms_scale(y) * gf_ref[...]


def _mlp(x, gain, w_up, w_down, layer, gain_final, *, final_norm, tm=512, tf=1024, name="mlp"):
    m, d = x.shape
    dff = w_up.shape[2]
    tm = min(tm, m)
    assert m % tm == 0 and dff % tf == 0
    return pl.pallas_call(
        functools.partial(_mlp_kernel, final_norm=final_norm),
        grid=(m // tm, dff // tf),
        in_specs=[
            pl.BlockSpec((tm, d), lambda i, c: (i, 0)),
            pl.BlockSpec((1, d), lambda i, c: (0, 0)),
            pl.BlockSpec((None, d, tf), lambda i, c: (layer, 0, c)),
            pl.BlockSpec((None, tf, d), lambda i, c: (layer, c, 0)),
            pl.BlockSpec((1, d), lambda i, c: (0, 0)),
        ],
        out_specs=pl.BlockSpec((tm, d), lambda i, c: (i, 0)),
        out_shape=jax.ShapeDtypeStruct((m, d), F32),
        scratch_shapes=[pltpu.VMEM((tm, d), BF16)],
        compiler_params=_cparams(2), name=name,
    )(x, gain.reshape(1, d).astype(F32), w_up, w_down, gain_final.reshape(1, d).astype(F32))


def _softplus2(z2):
    neg_abs = lax.bitcast_convert_type(lax.bitcast_convert_type(z2, jnp.uint32) | jnp.uint32(0x80000000), F32)
    return jnp.maximum(z2, 0.0) + jnp.log2(1.0 + jnp.exp2(neg_abs))


def _sb_block(q, kb, vb, ntri, bias2, carry, allowed_top):
    def mask_top(x):
        if allowed_top is None:
            return x
        nt = allowed_top.shape[0]
        top = jnp.where(allowed_top, x[:nt], 0.0)
        return top if x.shape[0] == nt else jnp.concatenate([top, x[nt:]], axis=0)

    z = lax.dot_general(q, kb.astype(BF16), (((1,), (1,)), ((), ())), preferred_element_type=F32) + bias2
    sp = mask_top(_softplus2(z))
    cum = jnp.dot(sp.astype(BF16), ntri, preferred_element_type=F32)
    w = mask_top(jnp.exp2((z + carry) + cum))
    pv = jnp.dot(w.astype(BF16), vb.astype(BF16), preferred_element_type=F32)
    return pv, carry + cum[:, 0:1]


def _sb_prompt_kernel(bias_ref, q_ref, k_ref, v_ref, ntri_ref, o_ref, acc_ref, carry_ref, *, bq, bk, unroll):
    h = pl.program_id(0)
    qi = pl.program_id(1)
    bias2 = bias_ref[h] * LOG2E
    ntri = ntri_ref[...]
    n_sub = bq // bk
    row = lax.broadcasted_iota(jnp.int32, (bk, bk), 0)
    col = lax.broadcasted_iota(jnp.int32, (bk, bk), 1)
    allowed_top = col < row

    for c in reversed(range(n_sub)):
        r0 = c * bk
        k0 = pl.multiple_of(qi * bq + r0, bk)
        if c == n_sub - 1:
            carry_in = jnp.zeros((bk, 1), F32)
        else:
            carry_in = jnp.concatenate([jnp.zeros((bk, 1), F32), carry_ref[r0 + bk:, :]], axis=0)
        pv, carry = _sb_block(q_ref[r0:, :], k_ref[pl.ds(k0, bk), :], v_ref[pl.ds(k0, bk), :], ntri, bias2,
                              carry_in, allowed_top)
        acc_ref[r0:r0 + bk, :] = pv[:bk]
        if c < n_sub - 1:
            acc_ref[r0 + bk:, :] += pv[bk:]
        carry_ref[r0:, :] = carry

    def body(t, _):
        carry = carry_ref[...]
        pv_sum = None
        for u in range(unroll):
            k0 = pl.multiple_of(qi * bq - (unroll * t + u + 1) * bk, bk)
            pv, carry = _sb_block(q_ref[...], k_ref[pl.ds(k0, bk), :], v_ref[pl.ds(k0, bk), :], ntri, bias2,
                                  carry, None)
            pv_sum = pv if pv_sum is None else pv_sum + pv
        acc_ref[...] += pv_sum
        carry_ref[...] = carry
        return 0

    assert n_sub % unroll == 0
    lax.fori_loop(0, qi * (n_sub // unroll), body, 0)
    o_ref[...] = acc_ref[...].astype(o_ref.dtype)


def _tri_ge(n, value=1.0):
    i = np.arange(n)
    return jnp.asarray((i[:, None] >= i[None, :]).astype(np.float32) * value, dtype=BF16)


def _sb_attention_prompt(q, k, v, bias, *, bq=1024, bk=256, unroll=4):
    t = q.shape[0]
    dh = SB_HEAD_DIM
    return pl.pallas_call(
        functools.partial(_sb_prompt_kernel, bq=bq, bk=bk, unroll=unroll),
        grid=(SB_HEADS, t // bq),
        in_specs=[
            pl.BlockSpec(memory_space=pltpu.SMEM),
            pl.BlockSpec((bq, dh), lambda h, i: (i, h)),
            pl.BlockSpec((t, dh), lambda h, i: (0, h)),
            pl.BlockSpec((t, dh), lambda h, i: (0, h)),
            pl.BlockSpec((bk, bk), lambda h, i: (0, 0)),
        ],
        out_specs=pl.BlockSpec((bq, dh), lambda h, i: (i, h)),
        out_shape=jax.ShapeDtypeStruct((t, SB_HEADS * dh), BF16),
        scratch_shapes=[pltpu.VMEM((bq, dh), F32), pltpu.VMEM((bq, 1), F32)],
        compiler_params=_cparams(2), name="sb_attn_prompt",
    )(bias.astype(F32), q, k, v, _tri_ge(bk, -1.0))


def _sb_sample_kernel(pt_ref, q_ref, k2_ref, v2_ref, *rest, n_q, page, pps):
    k_refs, v_refs = rest[:pps], rest[pps:2 * pps]
    ntri_ref, npar_ref, bias_ref, o_ref, acc_ref, carry_ref, kpk_ref, vpk_ref = rest[2 * pps:]
    s = pl.program_id(1)
    dh = SB_HEAD_DIM
    n_pairs = SB_HEADS // 2
    prow = 2 * n_q
    nrow = SB_HEADS * n_q
    width = 2 * page

    q_rows = jnp.concatenate([q_ref[:, h * dh:(h + 1) * dh] for h in range(SB_HEADS)], axis=0).astype(BF16)
    row = lax.broadcasted_iota(jnp.int32, (nrow, width), 0)
    col = lax.broadcasted_iota(jnp.int32, (nrow, width), 1)
    head_parity = (row // n_q) % 2

    def step(k_of_pair, v_of_pair, allowed, cum_ref, total_cols):
        z = jnp.concatenate(
            [lax.dot_general(q_rows[a * prow:(a + 1) * prow], k_of_pair(a), (((1,), (1,)), ((), ())),
                             preferred_element_type=F32) for a in range(n_pairs)], axis=0) + bias_ref[...]
        sp = jnp.where(allowed, _softplus2(z), 0.0)
        cum = jnp.dot(sp.astype(BF16), cum_ref[...], preferred_element_type=F32)
        w = jnp.where(allowed, jnp.exp2(z + cum + carry_ref[...]), 0.0)
        total = cum[:, 0:1]
        for c in range(1, total_cols):
            total = total + cum[:, c:c + 1]
        carry_ref[...] += total
        wb = w.astype(BF16)
        for a in range(n_pairs):
            acc_ref[a * prow:(a + 1) * prow, :] += jnp.dot(wb[a * prow:(a + 1) * prow], v_of_pair(a),
                                                           preferred_element_type=F32)

    @pl.when(s == 0)
    def _():
        acc_ref[...] = jnp.zeros_like(acc_ref)
        carry_ref[...] = jnp.zeros_like(carry_ref)
        pad = jnp.zeros((width - prow, dh), F32)

        def new_pair(ref):
            return lambda a: jnp.concatenate([ref[:, 2 * a * dh:(2 * a + 1) * dh],
                                              ref[:, (2 * a + 1) * dh:(2 * a + 2) * dh], pad], axis=0).astype(BF16)

        allowed = (col // n_q == head_parity) & (col % n_q < row % n_q)
        step(new_pair(k2_ref), new_pair(v2_ref), allowed, ntri_ref, 1)

    @pl.when(s > 0)
    def _():
        allowed = col % 2 == head_parity
        for p in range(pps):
            kpk_ref[p] = pltpu.bitcast(k_refs[p][...].astype(BF16), jnp.uint32)
            vpk_ref[p] = pltpu.bitcast(v_refs[p][...].astype(BF16), jnp.uint32)

            def cache_pair(ref, p=p):
                return lambda a: pltpu.bitcast(ref[p, pl.ds(a, page, stride=n_pairs), :], BF16)

            step(cache_pair(kpk_ref), cache_pair(vpk_ref), allowed, npar_ref, 2)

    @pl.when(s == pl.num_programs(1) - 1)
    def _():
        for h in range(SB_HEADS):
            o_ref[:, h * dh:(h + 1) * dh] = acc_ref[h * n_q:(h + 1) * n_q, :]


def _sb_attention_sample(q, k2, v2, cache_k, cache_v, page_table, bias, *, n_q, page):
    nb, n_pages = page_table.shape
    pps = SAMPLE_PAGES_PER_STEP
    assert n_pages % pps == 0 and SB_HEADS % 2 == 0 and 2 * n_q <= 2 * page
    hd = SB_HEADS * SB_HEAD_DIM
    nrow = SB_HEADS * n_q
    prow = page * SB_HEADS
    width = 2 * page

    def cache_map(p):
        def index_map(b, s, pt):
            return (pt[b, n_pages - pps * (jnp.maximum(s, 1) - 1) - 1 - p], 0, 0)
        return index_map

    i = np.arange(width)
    ge = i[:, None] >= i[None, :]
    same_parity = (i[:, None] - i[None, :]) % 2 == 0
    ntri = jnp.asarray(-(ge).astype(np.float32), dtype=BF16)
    npar = jnp.asarray(-(ge & same_parity).astype(np.float32), dtype=BF16)
    bias_rows = jnp.broadcast_to(jnp.repeat(bias.astype(F32) * LOG2E, n_q)[:, None], (nrow, width))
    small = pl.BlockSpec((n_q, hd), lambda b, s, pt: (b, 0))
    const = lambda shape: pl.BlockSpec(shape, lambda b, s, pt: (0, 0))
    grid_spec = pltpu.PrefetchScalarGridSpec(
        num_scalar_prefetch=1,
        grid=(nb, n_pages // pps + 1),
        in_specs=([small, small, small]
                  + [pl.BlockSpec((None, prow, SB_HEAD_DIM), cache_map(p)) for p in range(pps)] * 2
                  + [const((width, width)), const((width, width)), const((nrow, width))]),
        out_specs=small,
        scratch_shapes=[pltpu.VMEM((nrow, SB_HEAD_DIM), F32), pltpu.VMEM((nrow, 1), F32),
                        pltpu.VMEM((pps, prow // 2, SB_HEAD_DIM), jnp.uint32),
                        pltpu.VMEM((pps, prow // 2, SB_HEAD_DIM), jnp.uint32)],
    )
    return pl.pallas_call(
        functools.partial(_sb_sample_kernel, n_q=n_q, page=page, pps=pps),
        grid_spec=grid_spec,
        out_shape=jax.ShapeDtypeStruct((nb * n_q, hd), F32),
        compiler_params=_cparams(2), name="sb_attn_sample",
    )(page_table, q, k2, v2, *([cache_k] * pps), *([cache_v] * pps), ntri, npar, bias_rows)


def _gla_masks(r, c, sc):
    i = np.arange(r)
    chunk = i // c
    cstart = chunk * c
    sub0 = (i // sc) * sc
    same = chunk[:, None] == chunk[None, :]
    il, ll = i[:, None], i[None, :]
    nb = c // sc
    mats = [
        same & (ll <= il),
        (sub0[:, None] < ll) & (ll <= il),
        same & (il < ll),
    ]
    for blk in range(1, nb):
        mats.append(same & (il < ll) & (ll <= (cstart + sc * blk)[:, None]))
    sums = np.concatenate(mats, axis=0).astype(np.float32)
    sel = []
    for blk in range(1, nb):
        sel.append(same & (((i % c) // sc) == blk)[:, None] & (ll < (cstart + sc * blk)[:, None]))
    sel = np.stack(sel, axis=0).astype(np.float32) if sel else np.zeros((1, r, r), np.float32)
    return jnp.asarray(sums, dtype=BF16), jnp.asarray(sel, dtype=F32)


def _gla_prompt_kernel(qk_ref, v_ref, r_ref, lr_ref, wa_ref, ba_ref, gn_ref, sums_ref, sel_ref,
                       o_ref, s_out_ref, s_ref, g_scr, k_scr, a_scr, *, rows, chunk, sub):
    t = pl.program_id(0)
    nk = GLA_HEADS * GLA_DK
    n_chunks = rows // chunk
    n_blk = chunk // sub
    n_sub = rows // sub

    @pl.when(t == 0)
    def _():
        s_ref[...] = jnp.zeros_like(s_ref)

    lr = lr_ref[...].astype(BF16)
    for h in range(GLA_HEADS):
        ksl = slice(h * GLA_DK, (h + 1) * GLA_DK)
        vsl = slice(h * GLA_DV, (h + 1) * GLA_DV)
        x = jnp.dot(lr, wa_ref[:, ksl], preferred_element_type=F32) + ba_ref[:, ksl]
        g = _log2_gate(x)
        g_hi = g.astype(BF16)
        g_lo = (g - g_hi.astype(F32)).astype(BF16)
        e = (jnp.dot(sums_ref[...], g_hi, preferred_element_type=F32)
             + jnp.dot(sums_ref[...], g_lo, preferred_element_type=F32))
        gcum = e[0:rows]
        qs = qk_ref[:, ksl].astype(F32) * (GLA_DK ** -0.5)
        kf = qk_ref[:, nk + h * GLA_DK: nk + (h + 1) * GLA_DK].astype(F32)
        vb = v_ref[:, vsl]
        q_hat = (qs * jnp.exp2(gcum)).astype(BF16)
        q_til = (qs * jnp.exp2(e[rows:2 * rows])).astype(BF16)
        k_hat = (kf * jnp.exp2(e[2 * rows:3 * rows])).astype(BF16)

        a = jnp.zeros((rows, rows), F32)
        for b in range(1, n_blk):
            k_til = (kf * jnp.exp2(e[(2 + b) * rows:(3 + b) * rows])).astype(BF16)
            p = lax.dot_general(q_til, k_til, (((1,), (1,)), ((), ())), preferred_element_type=F32)
            a = a + p * sel_ref[b - 1]
        a_scr[...] = a

        g_scr[...] = gcum
        k_scr[...] = kf
        lane = lax.broadcasted_iota(jnp.int32, (sub, LANES), 1)
        rowi = lax.broadcasted_iota(jnp.int32, (sub, 1), 0)
        for sci in range(n_sub):
            r0 = sci * sub
            q_s = qs[r0:r0 + sub]
            g_s = gcum[r0:r0 + sub]
            lane0 = r0 % LANES
            tile = r0 // LANES
            blk_a = jnp.zeros((sub, LANES), F32)
            for j in range(sub):
                kj = k_scr[r0 + j:r0 + j + 1, :]
                gj = g_scr[r0 + j:r0 + j + 1, :]
                tt = q_s * kj * jnp.exp2(g_s - gj)
                colv = jnp.sum(tt, axis=-1, keepdims=True)
                colv = jnp.where(rowi >= j, colv, 0.0)
                blk_a = jnp.where(lane == lane0 + j, colv, blk_a)
            a_scr[r0:r0 + sub, tile * LANES:(tile + 1) * LANES] += blk_a

        o_intra = jnp.dot(a_scr[...].astype(BF16), vb, preferred_element_type=F32)

        gcum_t = gcum.T
        outs = []
        for c in range(n_chunks):
            c0 = c * chunk
            s_old = s_ref[h]
            o_c = o_intra[c0:c0 + chunk] + jnp.dot(q_hat[c0:c0 + chunk], s_old.astype(BF16),
                                                  preferred_element_type=F32)
            decay = jnp.exp2(gcum_t[:, c0 + chunk - 1:c0 + chunk])
            kv = lax.dot_general(k_hat[c0:c0 + chunk], vb[c0:c0 + chunk], (((0,), (0,)), ((), ())),
                                 preferred_element_type=F32)
            s_ref[h] = decay * s_old + kv
            outs.append(o_c)
        o_h = jnp.concatenate(outs, axis=0) if n_chunks > 1 else outs[0]
        o_n = o_h * _rms_scale(o_h) * gn_ref[...]
        r_h = r_ref[:, vsl].astype(F32)
        o_ref[:, vsl] = (o_n * (r_h / (1.0 + jnp.exp(-r_h)))).astype(o_ref.dtype)

    @pl.when(t == pl.num_programs(0) - 1)
    def _():
        s_out_ref[...] = s_ref[...]


def _gla_prompt(p, lr, w_a2p, b_a, g_norm, *, rows=256):
    t = p.shape[0]
    nv = GLA_HEADS * GLA_DV
    sums, sel = _gla_masks(rows, GLA_CHUNK, GLA_SUB)
    n_sums = sums.shape[0]
    o, s = pl.pallas_call(
        functools.partial(_gla_prompt_kernel, rows=rows, chunk=GLA_CHUNK, sub=GLA_SUB),
        grid=(t // rows,),
        in_specs=[
            pl.BlockSpec((rows, nv), lambda i: (i, 0)),
            pl.BlockSpec((rows, nv), lambda i: (i, 1)),
            pl.BlockSpec((rows, nv), lambda i: (i, 2)),
            pl.BlockSpec((rows, LANES), lambda i: (i, 0)),
            pl.BlockSpec((LANES, GLA_HEADS * GLA_DK), lambda i: (0, 0)),
            pl.BlockSpec((1, GLA_HEADS * GLA_DK), lambda i: (0, 0)),
            pl.BlockSpec((1, GLA_DV), lambda i: (0, 0)),
            pl.BlockSpec((n_sums, rows), lambda i: (0, 0)),
            pl.BlockSpec(sel.shape, lambda i: (0, 0, 0)),
        ],
        out_specs=[
            pl.BlockSpec((rows, nv), lambda i: (i, 0)),
            pl.BlockSpec((GLA_HEADS, GLA_DK, GLA_DV), lambda i: (0, 0, 0)),
        ],
        out_shape=[
            jax.ShapeDtypeStruct((t, nv), BF16),
            jax.ShapeDtypeStruct((GLA_HEADS, GLA_DK, GLA_DV), F32),
        ],
        scratch_shapes=[
            pltpu.VMEM((GLA_HEADS, GLA_DK, GLA_DV), F32),
            pltpu.VMEM((rows, GLA_DK), F32),
            pltpu.VMEM((rows, GLA_DK), F32),
            pltpu.VMEM((rows, rows), F32),
        ],
        compiler_params=_cparams(1), name="gla_prompt",
    )(p, p, p, lr, w_a2p, b_a.reshape(1, -1).astype(F32), g_norm.reshape(1, -1).astype(F32), sums, sel)
    return o, s


def _gla_sample_kernel(q_ref, k_ref, v_ref, r_ref, lr_ref, wa_ref, ba_ref, gn_ref, s0_ref, o_ref, s_out_ref,
                       *, n_t):
    pad = jnp.zeros((n_t, LANES), F32)
    lr16 = jnp.concatenate([lr_ref[...], pad], axis=0).astype(BF16)
    for h in range(GLA_HEADS):
        ksl = slice(h * GLA_DK, (h + 1) * GLA_DK)
        vsl = slice(h * GLA_DV, (h + 1) * GLA_DV)
        _gla_sample_head(q_ref.at[:, ksl], k_ref.at[:, ksl], v_ref.at[:, vsl], r_ref.at[:, vsl], lr16,
                         wa_ref.at[:, ksl], ba_ref.at[:, ksl], gn_ref, s0_ref.at[h], o_ref.at[:, vsl],
                         s_out_ref.at[h], n_t=n_t)


def _gla_sample_head(q_ref, k_ref, v_ref, r_ref, lr16, wa_ref, ba_ref, gn_ref, s0_ref, o_ref, s_out_ref, *, n_t):
    x = jnp.dot(lr16, wa_ref[...], preferred_element_type=F32)[0:n_t] + ba_ref[...]
    g = _log2_gate(x)
    rowi = lax.broadcasted_iota(jnp.int32, (n_t, 1), 0)
    gcum = jnp.zeros_like(g)
    for l in range(n_t):
        gcum = gcum + jnp.where(rowi >= l, g[l:l + 1, :], 0.0)
    qs = q_ref[...].astype(F32) * (GLA_DK ** -0.5)
    kf = k_ref[...].astype(F32)
    vf = v_ref[...].astype(F32)
    s_old = s0_ref[...]

    o = jnp.zeros((n_t, GLA_DV), F32)
    for j in range(n_t):
        tt = qs * kf[j:j + 1, :] * jnp.exp2(gcum - gcum[j:j + 1, :])
        colv = jnp.where(rowi >= j, jnp.sum(tt, axis=-1, keepdims=True), 0.0)
        o = o + colv * vf[j:j + 1, :]

    zpad = jnp.zeros((n_t, GLA_DK), F32)
    q_hat = jnp.concatenate([qs * jnp.exp2(gcum), zpad], axis=0).astype(BF16)
    o = o + jnp.dot(q_hat, s_old.astype(BF16), preferred_element_type=F32)[0:n_t]

    g_last = gcum[n_t - 1:n_t, :]
    k_hat = jnp.concatenate([kf * jnp.exp2(g_last - gcum), zpad], axis=0).astype(BF16)
    v16 = jnp.concatenate([vf, jnp.zeros((n_t, GLA_DV), F32)], axis=0).astype(BF16)
    kv = lax.dot_general(k_hat, v16, (((0,), (0,)), ((), ())), preferred_element_type=F32)
    decay = jnp.exp2(jnp.broadcast_to(g_last, (LANES, GLA_DK)).T[:, 0:1])
    s_out_ref[...] = decay * s_old + kv

    o_n = o * _rms_scale(o) * gn_ref[...]
    r_h = r_ref[...].astype(F32)
    o_ref[...] = o_n * (r_h / (1.0 + jnp.exp(-r_h)))


def _gla_sample(p, lr, w_a2p, b_a, g_norm, s0, *, n_t, n_seq):
    nb = n_seq
    nk = GLA_HEADS * GLA_DK
    nv = GLA_HEADS * GLA_DV
    assert nv == 2 * nk
    o, s = pl.pallas_call(
        functools.partial(_gla_sample_kernel, n_t=n_t),
        grid=(nb,),
        in_specs=[
            pl.BlockSpec((n_t, nk), lambda b: (b, 0)),
            pl.BlockSpec((n_t, nk), lambda b: (b, 1)),
            pl.BlockSpec((n_t, nv), lambda b: (b, 1)),
            pl.BlockSpec((n_t, nv), lambda b: (b, 2)),
            pl.BlockSpec((n_t, LANES), lambda b: (b, 0)),
            pl.BlockSpec((LANES, nk), lambda b: (0, 0)),
            pl.BlockSpec((1, nk), lambda b: (0, 0)),
            pl.BlockSpec((1, GLA_DV), lambda b: (0, 0)),
            pl.BlockSpec((None, GLA_HEADS, GLA_DK, GLA_DV), lambda b: (b, 0, 0, 0)),
        ],
        out_specs=[
            pl.BlockSpec((n_t, nv), lambda b: (b, 0)),
            pl.BlockSpec((None, GLA_HEADS, GLA_DK, GLA_DV), lambda b: (b, 0, 0, 0)),
        ],
        out_shape=[
            jax.ShapeDtypeStruct((nb * n_t, nv), F32),
            jax.ShapeDtypeStruct((nb,) + s0.shape[1:], F32),
        ],
        compiler_params=_cparams(1), name="gla_sample",
    )(p, p, p, p, lr, w_a2p, b_a.reshape(1, -1).astype(F32), g_norm.reshape(1, -1).astype(F32), s0)
    return o, s


def _gla_w_split_kernel(w_ref, main_ref, lr_ref, *, n_qkv, rank):
    main_ref[:n_qkv, :] = w_ref[:n_qkv, :].astype(BF16)
    main_ref[n_qkv:, :] = w_ref[n_qkv + rank:, :].astype(BF16)
    lr_ref[...] = jnp.concatenate([w_ref[n_qkv:n_qkv + rank, :],
                                   jnp.zeros((LANES - rank, w_ref.shape[1]), F32)], axis=0).astype(BF16)


def _gla_split_w_in(w_in_t, layer, *, tk=512):
    _, n_in, d = w_in_t.shape
    n_qkv = 2 * GLA_HEADS * GLA_DK + GLA_HEADS * GLA_DV
    n_main = n_in - GLA_GATE_RANK
    assert d % tk == 0 and n_qkv % 16 == 0 and GLA_GATE_RANK % 16 == 0
    return pl.pallas_call(
        functools.partial(_gla_w_split_kernel, n_qkv=n_qkv, rank=GLA_GATE_RANK),
        grid=(d // tk,),
        in_specs=[pl.BlockSpec((None, n_in, tk), lambda i: (layer, 0, i))],
        out_specs=[pl.BlockSpec((n_main, tk), lambda i: (0, i)), pl.BlockSpec((LANES, tk), lambda i: (0, i))],
        out_shape=[jax.ShapeDtypeStruct((n_main, d), BF16), jax.ShapeDtypeStruct((LANES, d), BF16)],
        compiler_params=_cparams(1), name="gla_w_split",
    )(w_in_t)


def kernel(x_prompt, x_sample, cache_k, cache_v, page_table, state_gla, norm_mix, sb_w_qkv, sb_bias, sb_w_o,
           gla_w_in, gla_w_a2, gla_b_a, gla_g_norm, gla_w_o, norm_ffn, w_up, w_down, norm_final):
    bp, tp, d = x_prompt.shape
    bs, ts, _ = x_sample.shape
    assert bp == 1
    hd = SB_HEADS * SB_HEAD_DIM
    xp = x_prompt.reshape(bp * tp, d)
    xs = x_sample.reshape(bs * ts, d)
    scale = SB_HEAD_DIM ** -0.5 * LOG2E
    nk = GLA_HEADS * GLA_DK
    nv = GLA_HEADS * GLA_DV

    wqkv = sb_w_qkv.astype(BF16)
    wo = sb_w_o.astype(BF16)
    g0 = norm_mix[0]

    q_p, k_p, v_p = _proj(xp, wqkv, layer=0, gain=g0, tn=hd, name="sb_qkv_prompt",
                          outs=[(BF16, 0, hd, scale), (F32, hd, hd, None), (F32, 2 * hd, hd, None)])
    o_p = _sb_attention_prompt(q_p, k_p, v_p, sb_bias[0])
    (xp,) = _proj(o_p, wo, layer=0, res=xp, tm=1024, name="sb_o_prompt")

    q_s, k_s, v_s = _proj(xs, wqkv, layer=0, gain=g0, name="sb_qkv_sample",
                          outs=[(F32, 0, hd, scale), (F32, hd, hd, None), (F32, 2 * hd, hd, None)])
    page = cache_k.shape[2]
    o_s = _sb_attention_sample(q_s, k_s, v_s, cache_k.reshape(-1, page * SB_HEADS, SB_HEAD_DIM),
                               cache_v.reshape(-1, page * SB_HEADS, SB_HEAD_DIM), page_table, sb_bias[0],
                               n_q=ts, page=page)
    (xs,) = _proj(o_s, wo, layer=0, res=xs, name="sb_o_sample")

    wu, wd = w_up.astype(BF16), w_down.astype(BF16)
    xp = _mlp(xp, norm_ffn[0], wu, wd, 0, norm_final, final_norm=False, name="mlp0_prompt")
    xs = _mlp(xs, norm_ffn[0], wu, wd, 0, norm_final, final_norm=False, name="mlp0_sample")

    w_main, w_lr = _gla_split_w_in(jnp.swapaxes(gla_w_in, 1, 2), 0)
    w_a2p = jnp.pad(gla_w_a2[0], ((0, LANES - GLA_GATE_RANK), (0, 0))).astype(BF16)
    w_go = gla_w_o.astype(BF16)
    g1 = norm_mix[1]
    n_main = w_main.shape[0]

    (p_p,) = _proj(xp, w_main, w_t=True, gain=g1, outs=[(BF16, 0, n_main, None)], tn=2048, name="gla_in_prompt")
    (lr_p,) = _proj(xp, w_lr, w_t=True, gain=g1, name="gla_lr_prompt")
    go_p, s_p = _gla_prompt(p_p, lr_p, w_a2p, gla_b_a[0], gla_g_norm[0])
    (xp,) = _proj(go_p, w_go, layer=0, res=xp, tm=1024, name="gla_o_prompt")

    (p_s,) = _proj(xs, w_main, w_t=True, gain=g1, name="gla_in_sample")
    (lr_s,) = _proj(xs, w_lr, w_t=True, gain=g1, name="gla_lr_sample")
    s0 = state_gla.reshape(-1, GLA_HEADS, GLA_DK, GLA_DV)
    go_s, s_s = _gla_sample(p_s, lr_s, w_a2p, gla_b_a[0], gla_g_norm[0], s0, n_t=ts, n_seq=bs)
    (xs,) = _proj(go_s, w_go, layer=0, res=xs, name="gla_o_sample")

    y_p = _mlp(xp, norm_ffn[1], wu, wd, 1, norm_final, final_norm=True, name="mlp1_prompt")
    y_s = _mlp(xs, norm_ffn[1], wu, wd, 1, norm_final, final_norm=True, name="mlp1_sample")

    return (
        y_p.reshape(bp, tp, d),
        y_s.reshape(bs, ts, d),
        k_p.reshape(1, bp, tp, SB_HEADS, SB_HEAD_DIM),
        v_p.reshape(1, bp, tp, SB_HEADS, SB_HEAD_DIM),
        k_s.reshape(1, bs, ts, SB_HEADS, SB_HEAD_DIM),
        v_s.reshape(1, bs, ts, SB_HEADS, SB_HEAD_DIM),
        s_p.reshape(1, bp, GLA_HEADS, GLA_DK, GLA_DV),
        s_s.reshape(1, bs, GLA_HEADS, GLA_DK, GLA_DV),
    )
```

```python
import functools

import numpy as np
import jax
import jax.numpy as jnp
from jax import lax
from jax.experimental import pallas as pl
from jax.experimental.pallas import tpu as pltpu

F32 = jnp.float32
BF16 = jnp.bfloat16

EPS = 1e-6
SB_HEADS = 16
SB_HEAD_DIM = 128
GLA_HEADS = 4
GLA_DK = 256
GLA_DV = 512
GLA_GATE_RANK = 16
GLA_TAU = 16.0
GLA_CHUNK = 64
GLA_SUB = 16
SAMPLE_PAGES_PER_STEP = 8
LANES = 128
LOG2E = 1.4426950408889634
VMEM_LIMIT_BYTES = 56 * 1024 * 1024


def _cparams(n_grid_dims):
    return pltpu.CompilerParams(
        dimension_semantics=("arbitrary",) * n_grid_dims,
        vmem_limit_bytes=VMEM_LIMIT_BYTES)


def _log2_gate(x):
    softplus_neg = jnp.maximum(-x, 0.0) + jnp.log(1.0 + jnp.exp(-jnp.abs(x)))
    return softplus_neg * (-LOG2E / GLA_TAU)


def _rms_scale(x):
    return lax.rsqrt(jnp.mean(x * x, axis=-1, keepdims=True) + EPS)


def _proj_kernel(*refs, has_norm, has_res, has_side, sections, n_tiles, w_t):
    it = iter(refs)
    x_ref = next(it)
    g_ref = next(it) if has_norm else None
    w_ref = next(it)
    side_w_ref = next(it) if has_side else None
    res_ref = next(it) if has_res else None
    out_refs = [next(it) for _ in sections]
    side_o_ref = next(it) if has_side else None
    h_ref = next(it) if has_norm else None
    j = pl.program_id(1)

    if has_norm:
        @pl.when(j == 0)
        def _():
            x = x_ref[...]
            h_ref[...] = (x * _rms_scale(x) * g_ref[...]).astype(BF16)
        h = h_ref[...]
    else:
        h = x_ref[...].astype(BF16)
    if has_side:
        @pl.when(j == 0)
        def _():
            side_o_ref[...] = lax.dot_general(h, side_w_ref[...], (((1,), (1,)), ((), ())),
                                              preferred_element_type=F32)
    if w_t:
        y = lax.dot_general(h, w_ref[...], (((1,), (1,)), ((), ())), preferred_element_type=F32)
    else:
        y = jnp.dot(h, w_ref[...], preferred_element_type=F32)
    if has_res:
        y = y + res_ref[...]
    for (first, count, scale), o_ref in zip(sections, out_refs):
        def write(o_ref=o_ref, scale=scale):
            o_ref[...] = (y if scale is None else y * scale).astype(o_ref.dtype)
        if first == 0 and count == n_tiles:
            write()
        else:
            pl.when((j >= first) & (j < first + count))(write)


def _proj(x, w, *, gain=None, res=None, outs=None, layer=None, w_t=False, side_w_t=None, tm=512, tn=1024,
          name="proj"):
    m, kdim = x.shape
    n = w.shape[-2] if w_t else w.shape[-1]
    assert not (w_t and layer is not None)
    tm = min(tm, m)
    tn = min(tn, n)
    assert m % tm == 0 and n % tn == 0
    n_tiles = n // tn
    outs = outs or [(F32, 0, n, None)]
    has_norm = gain is not None
    has_res = res is not None
    assert not has_res or (len(outs) == 1 and outs[0][1] == 0 and outs[0][2] == n)
    in_specs = [pl.BlockSpec((tm, kdim), lambda i, j: (i, 0))]
    args = [x]
    if has_norm:
        in_specs.append(pl.BlockSpec((1, kdim), lambda i, j: (0, 0)))
        args.append(gain.reshape(1, kdim).astype(F32))
    if w_t:
        in_specs.append(pl.BlockSpec((tn, kdim), lambda i, j: (j, 0)))
    elif layer is None:
        in_specs.append(pl.BlockSpec((kdim, tn), lambda i, j: (0, j)))
    else:
        in_specs.append(pl.BlockSpec((None, kdim, tn), lambda i, j: (layer, 0, j)))
    args.append(w)
    has_side = side_w_t is not None
    if has_side:
        in_specs.append(pl.BlockSpec(side_w_t.shape, lambda i, j: (0, 0)))
        args.append(side_w_t)
    if has_res:
        in_specs.append(pl.BlockSpec((tm, tn), lambda i, j: (i, j)))
        args.append(res)
    sections, out_shape, out_specs = [], [], []
    for dt, col0, cols, scale in outs:
        assert col0 % tn == 0 and cols % tn == 0
        first, count = col0 // tn, cols // tn
        sections.append((first, count, scale))
        out_shape.append(jax.ShapeDtypeStruct((m, cols), dt))
        out_specs.append(pl.BlockSpec(
            (tm, tn), lambda i, j, first=first, count=count: (i, jnp.clip(j - first, 0, count - 1))))
    if has_side:
        out_shape.append(jax.ShapeDtypeStruct((m, side_w_t.shape[0]), F32))
        out_specs.append(pl.BlockSpec((tm, side_w_t.shape[0]), lambda i, j: (i, 0)))
    scratch = [pltpu.VMEM((tm, kdim), BF16)] if has_norm else []
    return pl.pallas_call(
        functools.partial(_proj_kernel, has_norm=has_norm, has_res=has_res, has_side=has_side,
                          sections=tuple(sections), n_tiles=n_tiles, w_t=w_t),
        grid=(m // tm, n_tiles),
        in_specs=in_specs, out_specs=out_specs, out_shape=out_shape,
        scratch_shapes=scratch, compiler_params=_cparams(2), name=name,
    )(*args)


def _mlp_kernel(x_ref, g_ref, wu_ref, wd_ref, gf_ref, o_ref, h_ref, *, final_norm):
    c = pl.program_id(1)

    @pl.when(c == 0)
    def _():
        x = x_ref[...]
        h_ref[...] = (x * _rms_scale(x) * g_ref[...]).astype(BF16)
        o_ref[...] = x

    u = jnp.maximum(jnp.dot(h_ref[...], wu_ref[...], preferred_element_type=F32), 0.0)
    o_ref[...] += jnp.dot((u * u).astype(BF16), wd_ref[...], preferred_element_type=F32)

    if final_norm:
        @pl.when(c == pl.num_programs(1) - 1)
        def _():
            y = o_ref[...]
            o_ref[...] = y * _rms_scale(y) * gf_ref[...]


def _mlp(x, gain, w_up, w_down, layer, gain_final, *, final_norm, tm=512, tf=1024, name="mlp"):
    m, d = x.shape
    dff = w_up.shape[2]
    tm = min(tm, m)
    assert m % tm == 0 and dff % tf == 0
    return pl.pallas_call(
        functools.partial(_mlp_kernel, final_norm=final_norm),
        grid=(m // tm, dff // tf),
        in_specs=[
            pl.BlockSpec((tm, d), lambda i, c: (i, 0)),
            pl.BlockSpec((1, d), lambda i, c: (0, 0)),
            pl.BlockSpec((None, d, tf), lambda i, c: (layer, 0, c)),
            pl.BlockSpec((None, tf, d), lambda i, c: (layer, c, 0)),
            pl.BlockSpec((1, d), lambda i, c: (0, 0)),
        ],
        out_specs=pl.BlockSpec((tm, d), lambda i, c: (i, 0)),
        out_shape=jax.ShapeDtypeStruct((m, d), F32),
        scratch_shapes=[pltpu.VMEM((tm, d), BF16)],
        compiler_params=_cparams(2), name=name,
    )(x, gain.reshape(1, d).astype(F32), w_up, w_down, gain_final.reshape(1, d).astype(F32))


def _softplus2(z2):
    neg_abs = lax.bitcast_convert_type(lax.bitcast_convert_type(z2, jnp.uint32) | jnp.uint32(0x80000000), F32)
    return jnp.maximum(z2, 0.0) + jnp.log2(1.0 + jnp.exp2(neg_abs))


def _sb_block(q, kb, vb, ntri, bias2, carry, allowed_top):
    def mask_top(x):
        if allowed_top is None:
            return x
        nt = allowed_top.shape[0]
        top = jnp.where(allowed_top, x[:nt], 0.0)
        return top if x.shape[0] == nt else jnp.concatenate([top, x[nt:]], axis=0)

    z = lax.dot_general(q, kb.astype(BF16), (((1,), (1,)), ((), ())), preferred_element_type=F32) + bias2
    sp = mask_top(_softplus2(z))
    cum = jnp.dot(sp.astype(BF16), ntri, preferred_element_type=F32)
    w = mask_top(jnp.exp2((z + carry) + cum))
    pv = jnp.dot(w.astype(BF16), vb.astype(BF16), preferred_element_type=F32)
    return pv, carry + cum[:, 0:1]


def _sb_prompt_kernel(bias_ref, q_ref, k_ref, v_ref, ntri_ref, o_ref, acc_ref, carry_ref, *, bq, bk, unroll):
    h = pl.program_id(0)
    qi = pl.program_id(1)
    bias2 = bias_ref[h] * LOG2E
    ntri = ntri_ref[...]
    n_sub = bq // bk
    row = lax.broadcasted_iota(jnp.int32, (bk, bk), 0)
    col = lax.broadcasted_iota(jnp.int32, (bk, bk), 1)
    allowed_top = col < row

    for c in reversed(range(n_sub)):
        r0 = c * bk
        k0 = pl.multiple_of(qi * bq + r0, bk)
        if c == n_sub - 1:
            carry_in = jnp.zeros((bk, 1), F32)
        else:
            carry_in = jnp.concatenate([jnp.zeros((bk, 1), F32), carry_ref[r0 + bk:, :]], axis=0)
        pv, carry = _sb_block(q_ref[r0:, :], k_ref[pl.ds(k0, bk), :], v_ref[pl.ds(k0, bk), :], ntri, bias2,
                              carry_in, allowed_top)
        acc_ref[r0:r0 + bk, :] = pv[:bk]
        if c < n_sub - 1:
            acc_ref[r0 + bk:, :] += pv[bk:]
        carry_ref[r0:, :] = carry

    def body(t, _):
        carry = carry_ref[...]
        pv_sum = None
        for u in range(unroll):
            k0 = pl.multiple_of(qi * bq - (unroll * t + u + 1) * bk, bk)
            pv, carry = _sb_block(q_ref[...], k_ref[pl.ds(k0, bk), :], v_ref[pl.ds(k0, bk), :], ntri, bias2,
                                  carry, None)
            pv_sum = pv if pv_sum is None else pv_sum + pv
        acc_ref[...] += pv_sum
        carry_ref[...] = carry
        return 0

    assert n_sub % unroll == 0
    lax.fori_loop(0, qi * (n_sub // unroll), body, 0)
    o_ref[...] = acc_ref[...].astype(o_ref.dtype)


def _tri_ge(n, value=1.0):
    i = np.arange(n)
    return jnp.asarray((i[:, None] >= i[None, :]).astype(np.float32) * value, dtype=BF16)


def _sb_attention_prompt(q, k, v, bias, *, bq=1024, bk=256, unroll=4):
    t = q.shape[0]
    dh = SB_HEAD_DIM
    return pl.pallas_call(
        functools.partial(_sb_prompt_kernel, bq=bq, bk=bk, unroll=unroll),
        grid=(SB_HEADS, t // bq),
        in_specs=[
            pl.BlockSpec(memory_space=pltpu.SMEM),
            pl.BlockSpec((bq, dh), lambda h, i: (i, h)),
            pl.BlockSpec((t, dh), lambda h, i: (0, h)),
            pl.BlockSpec((t, dh), lambda h, i: (0, h)),
            pl.BlockSpec((bk, bk), lambda h, i: (0, 0)),
        ],
        out_specs=pl.BlockSpec((bq, dh), lambda h, i: (i, h)),
        out_shape=jax.ShapeDtypeStruct((t, SB_HEADS * dh), BF16),
        scratch_shapes=[pltpu.VMEM((bq, dh), F32), pltpu.VMEM((bq, 1), F32)],
        compiler_params=_cparams(2), name="sb_attn_prompt",
    )(bias.astype(F32), q, k, v, _tri_ge(bk, -1.0))


def _sb_sample_kernel(pt_ref, q_ref, k2_ref, v2_ref, *rest, n_q, page, pps):
    k_refs, v_refs = rest[:pps], rest[pps:2 * pps]
    ntri_ref, npar_ref, bias_ref, o_ref, acc_ref, carry_ref, kpk_ref, vpk_ref = rest[2 * pps:]
    s = pl.program_id(1)
    dh = SB_HEAD_DIM
    n_pairs = SB_HEADS // 2
    prow = 2 * n_q
    nrow = SB_HEADS * n_q
    width = 2 * page

    q_rows = jnp.concatenate([q_ref[:, h * dh:(h + 1) * dh] for h in range(SB_HEADS)], axis=0).astype(BF16)
    row = lax.broadcasted_iota(jnp.int32, (nrow, width), 0)
    col = lax.broadcasted_iota(jnp.int32, (nrow, width), 1)
    head_parity = (row // n_q) % 2

    def step(k_of_pair, v_of_pair, allowed, cum_ref, total_cols):
        z = jnp.concatenate(
            [lax.dot_general(q_rows[a * prow:(a + 1) * prow], k_of_pair(a), (((1,), (1,)), ((), ())),
                             preferred_element_type=F32) for a in range(n_pairs)], axis=0) + bias_ref[...]
        sp = jnp.where(allowed, _softplus2(z), 0.0)
        cum = jnp.dot(sp.astype(BF16), cum_ref[...], preferred_element_type=F32)
        w = jnp.where(allowed, jnp.exp2(z + cum + carry_ref[...]), 0.0)
        total = cum[:, 0:1]
        for c in range(1, total_cols):
            total = total + cum[:, c:c + 1]
        carry_ref[...] += total
        wb = w.astype(BF16)
        for a in range(n_pairs):
            acc_ref[a * prow:(a + 1) * prow, :] += jnp.dot(wb[a * prow:(a + 1) * prow], v_of_pair(a),
                                                           preferred_element_type=F32)

    @pl.when(s == 0)
    def _():
        acc_ref[...] = jnp.zeros_like(acc_ref)
        carry_ref[...] = jnp.zeros_like(carry_ref)
        pad = jnp.zeros((width - prow, dh), F32)

        def new_pair(ref):
            return lambda a: jnp.concatenate([ref[:, 2 * a * dh:(2 * a + 1) * dh],
                                              ref[:, (2 * a + 1) * dh:(2 * a + 2) * dh], pad], axis=0).astype(BF16)

        allowed = (col // n_q == head_parity) & (col % n_q < row % n_q)
        step(new_pair(k2_ref), new_pair(v2_ref), allowed, ntri_ref, 1)

    @pl.when(s > 0)
    def _():
        allowed = col % 2 == head_parity
        for p in range(pps):
            kpk_ref[p] = pltpu.bitcast(k_refs[p][...].astype(BF16), jnp.uint32)
            vpk_ref[p] = pltpu.bitcast(v_refs[p][...].astype(BF16), jnp.uint32)

            def cache_pair(ref, p=p):
                return lambda a: pltpu.bitcast(ref[p, pl.ds(a, page, stride=n_pairs), :], BF16)

            step(cache_pair(kpk_ref), cache_pair(vpk_ref), allowed, npar_ref, 2)

    @pl.when(s == pl.num_programs(1) - 1)
    def _():
        for h in range(SB_HEADS):
            o_ref[:, h * dh:(h + 1) * dh] = acc_ref[h * n_q:(h + 1) * n_q, :]


def _sb_attention_sample(q, k2, v2, cache_k, cache_v, page_table, bias, *, n_q, page):
    nb, n_pages = page_table.shape
    pps = SAMPLE_PAGES_PER_STEP
    assert n_pages % pps == 0 and SB_HEADS % 2 == 0 and 2 * n_q <= 2 * page
    hd = SB_HEADS * SB_HEAD_DIM
    nrow = SB_HEADS * n_q
    prow = page * SB_HEADS
    width = 2 * page

    def cache_map(p):
        def index_map(b, s, pt):
            return (pt[b, n_pages - pps * (jnp.maximum(s, 1) - 1) - 1 - p], 0, 0)
        return index_map

    i = np.arange(width)
    ge = i[:, None] >= i[None, :]
    same_parity = (i[:, None] - i[None, :]) % 2 == 0
    ntri = jnp.asarray(-(ge).astype(np.float32), dtype=BF16)
    npar = jnp.asarray(-(ge & same_parity).astype(np.float32), dtype=BF16)
    bias_rows = jnp.broadcast_to(jnp.repeat(bias.astype(F32) * LOG2E, n_q)[:, None], (nrow, width))
    small = pl.BlockSpec((n_q, hd), lambda b, s, pt: (b, 0))
    const = lambda shape: pl.BlockSpec(shape, lambda b, s, pt: (0, 0))
    grid_spec = pltpu.PrefetchScalarGridSpec(
        num_scalar_prefetch=1,
        grid=(nb, n_pages // pps + 1),
        in_specs=([small, small, small]
                  + [pl.BlockSpec((None, prow, SB_HEAD_DIM), cache_map(p)) for p in range(pps)] * 2
                  + [const((width, width)), const((width, width)), const((nrow, width))]),
        out_specs=small,
        scratch_shapes=[pltpu.VMEM((nrow, SB_HEAD_DIM), F32), pltpu.VMEM((nrow, 1), F32),
                        pltpu.VMEM((pps, prow // 2, SB_HEAD_DIM), jnp.uint32),
                        pltpu.VMEM((pps, prow // 2, SB_HEAD_DIM), jnp.uint32)],
    )
    return pl.pallas_call(
        functools.partial(_sb_sample_kernel, n_q=n_q, page=page, pps=pps),
        grid_spec=grid_spec,
        out_shape=jax.ShapeDtypeStruct((nb * n_q, hd), F32),
        compiler_params=_cparams(2), name="sb_attn_sample",
    )(page_table, q, k2, v2, *([cache_k] * pps), *([cache_v] * pps), ntri, npar, bias_rows)


def _gla_masks(r, c, sc):
    i = np.arange(r)
    chunk = i // c
    cstart = chunk * c
    sub0 = (i // sc) * sc
    same = chunk[:, None] == chunk[None, :]
    il, ll = i[:, None], i[None, :]
    nb = c // sc
    mats = [
        same & (ll <= il),
        (sub0[:, None] < ll) & (ll <= il),
        same & (il < ll),
    ]
    for blk in range(1, nb):
        mats.append(same & (il < ll) & (ll <= (cstart + sc * blk)[:, None]))
    sums = np.concatenate(mats, axis=0).astype(np.float32)
    sel = []
    for blk in range(1, nb):
        sel.append(same & (((i % c) // sc) == blk)[:, None] & (ll < (cstart + sc * blk)[:, None]))
    sel = np.stack(sel, axis=0).astype(np.float32) if sel else np.zeros((1, r, r), np.float32)
    return jnp.asarray(sums, dtype=BF16), jnp.asarray(sel, dtype=F32)


def _gla_prompt_kernel(qk_ref, v_ref, r_ref, lr_ref, wa_ref, ba_ref, gn_ref, sums_ref, sel_ref,
                       o_ref, s_out_ref, s_ref, g_scr, k_scr, a_scr, *, rows, chunk, sub):
    t = pl.program_id(0)
    nk = GLA_HEADS * GLA_DK
    n_chunks = rows // chunk
    n_blk = chunk // sub
    n_sub = rows // sub

    @pl.when(t == 0)
    def _():
        s_ref[...] = jnp.zeros_like(s_ref)

    lr = lr_ref[...].astype(BF16)
    for h in range(GLA_HEADS):
        ksl = slice(h * GLA_DK, (h + 1) * GLA_DK)
        vsl = slice(h * GLA_DV, (h + 1) * GLA_DV)
        x = jnp.dot(lr, wa_ref[:, ksl], preferred_element_type=F32) + ba_ref[:, ksl]
        g = _log2_gate(x)
        g_hi = g.astype(BF16)
        g_lo = (g - g_hi.astype(F32)).astype(BF16)
        e = (jnp.dot(sums_ref[...], g_hi, preferred_element_type=F32)
             + jnp.dot(sums_ref[...], g_lo, preferred_element_type=F32))
        gcum = e[0:rows]
        qs = qk_ref[:, ksl].astype(F32) * (GLA_DK ** -0.5)
        kf = qk_ref[:, nk + h * GLA_DK: nk + (h + 1) * GLA_DK].astype(F32)
        vb = v_ref[:, vsl]
        q_hat = (qs * jnp.exp2(gcum)).astype(BF16)
        q_til = (qs * jnp.exp2(e[rows:2 * rows])).astype(BF16)
        k_hat = (kf * jnp.exp2(e[2 * rows:3 * rows])).astype(BF16)

        a = jnp.zeros((rows, rows), F32)
        for b in range(1, n_blk):
            k_til = (kf * jnp.exp2(e[(2 + b) * rows:(3 + b) * rows])).astype(BF16)
            p = lax.dot_general(q_til, k_til, (((1,), (1,)), ((), ())), preferred_element_type=F32)
            a = a + p * sel_ref[b - 1]
        a_scr[...] = a

        g_scr[...] = gcum
        k_scr[...] = kf
        lane = lax.broadcasted_iota(jnp.int32, (sub, LANES), 1)
        rowi = lax.broadcasted_iota(jnp.int32, (sub, 1), 0)
        for sci in range(n_sub):
            r0 = sci * sub
            q_s = qs[r0:r0 + sub]
            g_s = gcum[r0:r0 + sub]
            lane0 = r0 % LANES
            tile = r0 // LANES
            blk_a = jnp.zeros((sub, LANES), F32)
            for j in range(sub):
                kj = k_scr[r0 + j:r0 + j + 1, :]
                gj = g_scr[r0 + j:r0 + j + 1, :]
                tt = q_s * kj * jnp.exp2(g_s - gj)
                colv = jnp.sum(tt, axis=-1, keepdims=True)
                colv = jnp.where(rowi >= j, colv, 0.0)
                blk_a = jnp.where(lane == lane0 + j, colv, blk_a)
            a_scr[r0:r0 + sub, tile * LANES:(tile + 1) * LANES] += blk_a

        o_intra = jnp.dot(a_scr[...].astype(BF16), vb, preferred_element_type=F32)

        gcum_t = gcum.T
        outs = []
        for c in range(n_chunks):
            c0 = c * chunk
            s_old = s_ref[h]
            o_c = o_intra[c0:c0 + chunk] + jnp.dot(q_hat[c0:c0 + chunk], s_old.astype(BF16),
                                                  preferred_element_type=F32)
            decay = jnp.exp2(gcum_t[:, c0 + chunk - 1:c0 + chunk])
            kv = lax.dot_general(k_hat[c0:c0 + chunk], vb[c0:c0 + chunk], (((0,), (0,)), ((), ())),
                                 preferred_element_type=F32)
            s_ref[h] = decay * s_old + kv
            outs.append(o_c)
        o_h = jnp.concatenate(outs, axis=0) if n_chunks > 1 else outs[0]
        o_n = o_h * _rms_scale(o_h) * gn_ref[...]
        r_h = r_ref[:, vsl].astype(F32)
        o_ref[:, vsl] = (o_n * (r_h / (1.0 + jnp.exp(-r_h)))).astype(o_ref.dtype)

    @pl.when(t == pl.num_programs(0) - 1)
    def _():
        s_out_ref[...] = s_ref[...]


def _gla_prompt(p, lr, w_a2p, b_a, g_norm, *, rows=256):
    t = p.shape[0]
    nv = GLA_HEADS * GLA_DV
    sums, sel = _gla_masks(rows, GLA_CHUNK, GLA_SUB)
    n_sums = sums.shape[0]
    o, s = pl.pallas_call(
        functools.partial(_gla_prompt_kernel, rows=rows, chunk=GLA_CHUNK, sub=GLA_SUB),
        grid=(t // rows,),
        in_specs=[
            pl.BlockSpec((rows, nv), lambda i: (i, 0)),
            pl.BlockSpec((rows, nv), lambda i: (i, 1)),
            pl.BlockSpec((rows, nv), lambda i: (i, 2)),
            pl.BlockSpec((rows, LANES), lambda i: (i, 0)),
            pl.BlockSpec((LANES, GLA_HEADS * GLA_DK), lambda i: (0, 0)),
            pl.BlockSpec((1, GLA_HEADS * GLA_DK), lambda i: (0, 0)),
            pl.BlockSpec((1, GLA_DV), lambda i: (0, 0)),
            pl.BlockSpec((n_sums, rows), lambda i: (0, 0)),
            pl.BlockSpec(sel.shape, lambda i: (0, 0, 0)),
        ],
        out_specs=[
            pl.BlockSpec((rows, nv), lambda i: (i, 0)),
            pl.BlockSpec((GLA_HEADS, GLA_DK, GLA_DV), lambda i: (0, 0, 0)),
        ],
        out_shape=[
            jax.ShapeDtypeStruct((t, nv), BF16),
            jax.ShapeDtypeStruct((GLA_HEADS, GLA_DK, GLA_DV), F32),
        ],
        scratch_shapes=[
            pltpu.VMEM((GLA_HEADS, GLA_DK, GLA_DV), F32),
            pltpu.VMEM((rows, GLA_DK), F32),
            pltpu.VMEM((rows, GLA_DK), F32),
            pltpu.VMEM((rows, rows), F32),
        ],
        compiler_params=_cparams(1), name="gla_prompt",
    )(p, p, p, lr, w_a2p, b_a.reshape(1, -1).astype(F32), g_norm.reshape(1, -1).astype(F32), sums, sel)
    return o, s


def _gla_sample_kernel(q_ref, k_ref, v_ref, r_ref, lr_ref, wa_ref, ba_ref, gn_ref, s0_ref, o_ref, s_out_ref,
                       *, n_t):
    pad = jnp.zeros((n_t, LANES), F32)
    lr16 = jnp.concatenate([lr_ref[...], pad], axis=0).astype(BF16)
    for h in range(GLA_HEADS):
        ksl = slice(h * GLA_DK, (h + 1) * GLA_DK)
        vsl = slice(h * GLA_DV, (h + 1) * GLA_DV)
        _gla_sample_head(q_ref.at[:, ksl], k_ref.at[:, ksl], v_ref.at[:, vsl], r_ref.at[:, vsl], lr16,
                         wa_ref.at[:, ksl], ba_ref.at[:, ksl], gn_ref, s0_ref.at[h], o_ref.at[:, vsl],
                         s_out_ref.at[h], n_t=n_t)


def _gla_sample_head(q_ref, k_ref, v_ref, r_ref, lr16, wa_ref, ba_ref, gn_ref, s0_ref, o_ref, s_out_ref, *, n_t):
    x = jnp.dot(lr16, wa_ref[...], preferred_element_type=F32)[0:n_t] + ba_ref[...]
    g = _log2_gate(x)
    rowi = lax.broadcasted_iota(jnp.int32, (n_t, 1), 0)
    gcum = jnp.zeros_like(g)
    for l in range(n_t):
        gcum = gcum + jnp.where(rowi >= l, g[l:l + 1, :], 0.0)
    qs = q_ref[...].astype(F32) * (GLA_DK ** -0.5)
    kf = k_ref[...].astype(F32)
    vf = v_ref[...].astype(F32)
    s_old = s0_ref[...]

    o = jnp.zeros((n_t, GLA_DV), F32)
    for j in range(n_t):
        tt = qs * kf[j:j + 1, :] * jnp.exp2(gcum - gcum[j:j + 1, :])
        colv = jnp.where(rowi >= j, jnp.sum(tt, axis=-1, keepdims=True), 0.0)
        o = o + colv * vf[j:j + 1, :]

    zpad = jnp.zeros((n_t, GLA_DK), F32)
    q_hat = jnp.concatenate([qs * jnp.exp2(gcum), zpad], axis=0).astype(BF16)
    o = o + jnp.dot(q_hat, s_old.astype(BF16), preferred_element_type=F32)[0:n_t]

    g_last = gcum[n_t - 1:n_t, :]
    k_hat = jnp.concatenate([kf * jnp.exp2(g_last - gcum), zpad], axis=0).astype(BF16)
    v16 = jnp.concatenate([vf, jnp.zeros((n_t, GLA_DV), F32)], axis=0).astype(BF16)
    kv = lax.dot_general(k_hat, v16, (((0,), (0,)), ((), ())), preferred_element_type=F32)
    decay = jnp.exp2(jnp.broadcast_to(g_last, (LANES, GLA_DK)).T[:, 0:1])
    s_out_ref[...] = decay * s_old + kv

    o_n = o * _rms_scale(o) * gn_ref[...]
    r_h = r_ref[...].astype(F32)
    o_ref[...] = o_n * (r_h / (1.0 + jnp.exp(-r_h)))


def _gla_sample(p, lr, w_a2p, b_a, g_norm, s0, *, n_t, n_seq):
    nb = n_seq
    nk = GLA_HEADS * GLA_DK
    nv = GLA_HEADS * GLA_DV
    assert nv == 2 * nk
    o, s = pl.pallas_call(
        functools.partial(_gla_sample_kernel, n_t=n_t),
        grid=(nb,),
        in_specs=[
            pl.BlockSpec((n_t, nk), lambda b: (b, 0)),
            pl.BlockSpec((n_t, nk), lambda b: (b, 1)),
            pl.BlockSpec((n_t, nv), lambda b: (b, 1)),
            pl.BlockSpec((n_t, nv), lambda b: (b, 2)),
            pl.BlockSpec((n_t, LANES), lambda b: (b, 0)),
            pl.BlockSpec((LANES, nk), lambda b: (0, 0)),
            pl.BlockSpec((1, nk), lambda b: (0, 0)),
            pl.BlockSpec((1, GLA_DV), lambda b: (0, 0)),
            pl.BlockSpec((None, GLA_HEADS, GLA_DK, GLA_DV), lambda b: (b, 0, 0, 0)),
        ],
        out_specs=[
            pl.BlockSpec((n_t, nv), lambda b: (b, 0)),
            pl.BlockSpec((None, GLA_HEADS, GLA_DK, GLA_DV), lambda b: (b, 0, 0, 0)),
        ],
        out_shape=[
            jax.ShapeDtypeStruct((nb * n_t, nv), F32),
            jax.ShapeDtypeStruct((nb,) + s0.shape[1:], F32),
        ],
        compiler_params=_cparams(1), name="gla_sample",
    )(p, p, p, p, lr, w_a2p, b_a.reshape(1, -1).astype(F32), g_norm.reshape(1, -1).astype(F32), s0)
    return o, s


def _gla_w_split_kernel(w_ref, main_ref, lr_ref, *, n_qkv, rank):
    main_ref[:n_qkv, :] = w_ref[:n_qkv, :].astype(BF16)
    main_ref[n_qkv:, :] = w_ref[n_qkv + rank:, :].astype(BF16)
    lr_ref[...] = jnp.concatenate([w_ref[n_qkv:n_qkv + rank, :],
                                   jnp.zeros((LANES - rank, w_ref.shape[1]), F32)], axis=0).astype(BF16)


def _gla_split_w_in(w_in_t, layer, *, tk=512):
    _, n_in, d = w_in_t.shape
    n_qkv = 2 * GLA_HEADS * GLA_DK + GLA_HEADS * GLA_DV
    n_main = n_in - GLA_GATE_RANK
    assert d % tk == 0 and n_qkv % 16 == 0 and GLA_GATE_RANK % 16 == 0
    return pl.pallas_call(
        functools.partial(_gla_w_split_kernel, n_qkv=n_qkv, rank=GLA_GATE_RANK),
        grid=(d // tk,),
        in_specs=[pl.BlockSpec((None, n_in, tk), lambda i: (layer, 0, i))],
        out_specs=[pl.BlockSpec((n_main, tk), lambda i: (0, i)), pl.BlockSpec((LANES, tk), lambda i: (0, i))],
        out_shape=[jax.ShapeDtypeStruct((n_main, d), BF16), jax.ShapeDtypeStruct((LANES, d), BF16)],
        compiler_params=_cparams(1), name="gla_w_split",
    )(w_in_t)


def kernel(x_prompt, x_sample, cache_k, cache_v, page_table, state_gla, norm_mix, sb_w_qkv, sb_bias, sb_w_o,
           gla_w_in, gla_w_a2, gla_b_a, gla_g_norm, gla_w_o, norm_ffn, w_up, w_down, norm_final):
    bp, tp, d = x_prompt.shape
    bs, ts, _ = x_sample.shape
    assert bp == 1
    hd = SB_HEADS * SB_HEAD_DIM
    xp = x_prompt.reshape(bp * tp, d)
    xs = x_sample.reshape(bs * ts, d)
    scale = SB_HEAD_DIM ** -0.5 * LOG2E
    nk = GLA_HEADS * GLA_DK
    nv = GLA_HEADS * GLA_DV

    wqkv = sb_w_qkv.astype(BF16)
    wo = sb_w_o.astype(BF16)
    g0 = norm_mix[0]

    q_p, k_p, v_p = _proj(xp, wqkv, layer=0, gain=g0, tn=hd, name="sb_qkv_prompt",
                          outs=[(BF16, 0, hd, scale), (F32, hd, hd, None), (F32, 2 * hd, hd, None)])
    o_p = _sb_attention_prompt(q_p, k_p, v_p, sb_bias[0])
    (xp,) = _proj(o_p, wo, layer=0, res=xp, tm=1024, name="sb_o_prompt")

    q_s, k_s, v_s = _proj(xs, wqkv, layer=0, gain=g0, name="sb_qkv_sample",
                          outs=[(F32, 0, hd, scale), (F32, hd, hd, None), (F32, 2 * hd, hd, None)])
    page = cache_k.shape[2]
    o_s = _sb_attention_sample(q_s, k_s, v_s, cache_k.reshape(-1, page * SB_HEADS, SB_HEAD_DIM),
                               cache_v.reshape(-1, page * SB_HEADS, SB_HEAD_DIM), page_table, sb_bias[0],
                               n_q=ts, page=page)
    (xs,) = _proj(o_s, wo, layer=0, res=xs, name="sb_o_sample")

    wu, wd = w_up.astype(BF16), w_down.astype(BF16)
    xp = _mlp(xp, norm_ffn[0], wu, wd, 0, norm_final, final_norm=False, name="mlp0_prompt")
    xs = _mlp(xs, norm_ffn[0], wu, wd, 0, norm_final, final_norm=False, name="mlp0_sample")

    w_main, w_lr = _gla_split_w_in(jnp.swapaxes(gla_w_in, 1, 2), 0)
    w_a2p = jnp.pad(gla_w_a2[0], ((0, LANES - GLA_GATE_RANK), (0, 0))).astype(BF16)
    w_go = gla_w_o.astype(BF16)
    g1 = norm_mix[1]
    n_main = w_main.shape[0]

    p_p, lr_p = _proj(xp, w_main, w_t=True, side_w_t=w_lr, gain=g1, outs=[(BF16, 0, n_main, None)], tn=2048,
                      name="gla_in_prompt")
    go_p, s_p = _gla_prompt(p_p, lr_p, w_a2p, gla_b_a[0], gla_g_norm[0])
    (xp,) = _proj(go_p, w_go, layer=0, res=xp, tm=1024, name="gla_o_prompt")

    p_s, lr_s = _proj(xs, w_main, w_t=True, side_w_t=w_lr, gain=g1, name="gla_in_sample")
    s0 = state_gla.reshape(-1, GLA_HEADS, GLA_DK, GLA_DV)
    go_s, s_s = _gla_sample(p_s, lr_s, w_a2p, gla_b_a[0], gla_g_norm[0], s0, n_t=ts, n_seq=bs)
    (xs,) = _proj(go_s, w_go, layer=0, res=xs, name="gla_o_sample")

    y_p = _mlp(xp, norm_ffn[1], wu, wd, 1, norm_final, final_norm=True, name="mlp1_prompt")
    y_s = _mlp(xs, norm_ffn[1], wu, wd, 1, norm_final, final_norm=True, name="mlp1_sample")

    return (
        y_p.reshape(bp, tp, d),
        y_s.reshape(bs, ts, d),
        k_p.reshape(1, bp, tp, SB_HEADS, SB_HEAD_DIM),
        v_p.reshape(1, bp, tp, SB_HEADS, SB_HEAD_DIM),
        k_s.reshape(1, bs, ts, SB_HEADS, SB_HEAD_DIM),
        v_s.reshape(1, bs, ts, SB_HEADS, SB_HEAD_DIM),
        s_p.reshape(1, bp, GLA_HEADS, GLA_DK, GLA_DV),
        s_s.reshape(1, bs, GLA_HEADS, GLA_DK, GLA_DV),
    )
```
